```python
import math
import jax
import jax.numpy as jnp
from jax import lax
import numpy as np

D_MODEL = 2048
BATCH = 16
SEQ = 256
DEPTH = 4
DEC_BATCH = 2
DEC_SEQ = 2048
PAST_LEN = 512

GRID_W = 64
N_MIXERS = 3
EPS = 1e-6

ML_INNER = 2 * D_MODEL
ML_HEADS = 8
ML_DV = ML_INNER // ML_HEADS
ML_DK = ML_DV // 2
ML_QK = ML_HEADS * ML_DK
ML_CHUNK = 64
ML_SPLITS = (ML_QK, 2 * ML_QK, 2 * ML_QK + ML_INNER, 2 * ML_QK + 2 * ML_INNER, 2 * ML_QK + 3 * ML_INNER)
ML_IN_COLS = 2 * ML_QK + 3 * ML_INNER + 4 * ML_HEADS

SC_INNER = 2 * D_MODEL
SC_IN_COLS = 4 * SC_INNER

GD_DK = 128
GD_DV = 128
GD_QK_HEADS = D_MODEL // GD_DK
GD_V_HEADS = 2 * GD_QK_HEADS
GD_QK = GD_QK_HEADS * GD_DK
GD_INNER = GD_V_HEADS * GD_DV
GD_CONV_CH = 2 * GD_QK + GD_INNER
GD_IN_COLS = GD_CONV_CH + GD_INNER + 4 * GD_V_HEADS
GD_CHUNK = 64

CONV_W = 3
N_ML = (DEPTH + 2) // 3
N_SC = (DEPTH + 1) // 3
N_GD = DEPTH // 3

kernel_name = "bidir_mlstm_shortconv_gdn_diffusion_step"


def rmsnorm(x, g):
    xf = x.astype(jnp.float32)
    xf = xf * lax.rsqrt(jnp.mean(jnp.square(xf), axis=-1, keepdims=True) + EPS)
    return (xf * g.astype(jnp.float32)).astype(x.dtype)


def dwconv3(x, w, rows):
    b, t, ch = x.shape
    xr = x.reshape(b, rows, t // rows, ch)
    xp = jnp.pad(xr, ((0, 0), (0, 0), (1, 1), (0, 0)))
    y = xp[:, :, :-2] * w[0] + xp[:, :, 1:-1] * w[1] + xp[:, :, 2:] * w[2]
    return y.reshape(b, t, ch)


def to_heads(a, nh):
    b, t, _ = a.shape
    return a.reshape(b, t, nh, -1).transpose(0, 2, 1, 3)


def flip_t(a):
    return jnp.flip(a, axis=2)


def mlstm_scan(q, k, v, ig, lf, C0, n0, m0):
    bsz, nh, t, _ = q.shape
    L = ML_CHUNK
    nc = t // L
    chunks = lambda a: jnp.moveaxis(a.reshape(bsz, nh, nc, L, *a.shape[3:]), 2, 0)
    causal = jnp.tril(jnp.ones((L, L), bool))

    def step(carry, inp):
        C, n, m = carry
        qc, kc, vc, ic, fc = inp
        b = jnp.cumsum(fc, axis=-1)
        dmat = jnp.where(causal, b[..., :, None] - b[..., None, :] + ic[..., None, :], -jnp.inf)
        inter = b + m[..., None]
        mt = jnp.maximum(inter, jnp.max(dmat, axis=-1))
        s = jnp.einsum('bhtk,bhsk->bhts', qc, kc) * jnp.exp(dmat - mt[..., None])
        sc = jnp.exp(inter - mt)
        num = sc[..., None] * jnp.einsum('bhtk,bhkv->bhtv', qc, C) + jnp.einsum('bhts,bhsv->bhtv', s, vc)
        den = sc * jnp.einsum('bhtk,bhk->bht', qc, n) + jnp.sum(s, axis=-1)
        h = num / jnp.maximum(jnp.abs(den), jnp.exp(-mt))[..., None]
        bl = b[..., -1]
        a = bl[..., None] - b + ic
        m_new = jnp.maximum(bl + m, jnp.max(a, axis=-1))
        w = jnp.exp(a - m_new[..., None])
        dec = jnp.exp(bl + m - m_new)
        C_new = dec[..., None, None] * C + jnp.einsum('bhs,bhsk,bhsv->bhkv', w, kc, vc)
        n_new = dec[..., None] * n + jnp.einsum('bhs,bhsk->bhk', w, kc)
        return (C_new, n_new, m_new), h

    (C, n, m), hs = lax.scan(step, (C0, n0, m0), (chunks(q), chunks(k), chunks(v), chunks(ig), chunks(lf)))
    return jnp.moveaxis(hs, 0, 2).reshape(bsz, nh, t, -1), C, n, m


def mlstm_mixer(h, w_in, b_gate, g_head, w_out, init):
    bsz, t, _ = h.shape
    f32 = jnp.float32
    q, k, v, o, z, gates = jnp.split(h @ w_in, ML_SPLITS, axis=-1)
    q = to_heads(q, ML_HEADS).astype(f32) * (ML_DK ** -0.5)
    k = to_heads(k, ML_HEADS).astype(f32)
    v = to_heads(v, ML_HEADS).astype(f32)
    gates = (gates + b_gate).astype(f32).reshape(bsz, t, 4, ML_HEADS).transpose(2, 0, 3, 1)
    ig_f, ig_b = gates[0], gates[1]
    lf_f, lf_b = jax.nn.log_sigmoid(gates[2]), jax.nn.log_sigmoid(gates[3])
    C0, n0, m0 = init
    h_f, C_f, n_f, m_f = mlstm_scan(q, k, v, ig_f, lf_f, C0[:, 0], n0[:, 0], m0[:, 0])
    h_b, C_b, n_b, m_b = mlstm_scan(flip_t(q), flip_t(k), flip_t(v), flip_t(ig_b), flip_t(lf_b),
                                    C0[:, 1], n0[:, 1], m0[:, 1])
    hs = (h_f + flip_t(h_b)).transpose(0, 2, 1, 3)
    hs = rmsnorm(hs, g_head.reshape(ML_HEADS, ML_DV)).reshape(bsz, t, ML_INNER).astype(h.dtype)
    y = hs * jax.nn.sigmoid(o) * jax.nn.silu(z)
    return y @ w_out, ((C_f, C_b), (n_f, n_b), (m_f, m_b))


def shortconv_mixer(h, w_in, w_conv, w_out, rows):
    u, bg, cg, z = jnp.split(h @ w_in, 4, axis=-1)
    y = bg * dwconv3(cg * u, w_conv, rows) * jax.nn.silu(z)
    return y @ w_out


def gdn_scan(q, k, v, beta, g, S0):
    bsz, nh, t, dk = q.shape
    L = GD_CHUNK
    nc = t // L
    rs = lambda a: a.reshape(bsz, nh, nc, L, *a.shape[3:])
    q, k, v, beta, g = rs(q), rs(k), rs(v), rs(beta), rs(g)
    G = jnp.cumsum(g, axis=-1)
    incl = jnp.tril(jnp.ones((L, L), bool))
    strict = jnp.tril(jnp.ones((L, L), bool), -1)
    decay = jnp.where(incl, jnp.exp(jnp.where(incl, G[..., :, None] - G[..., None, :], 0.0)), 0.0)
    A = jnp.where(strict, beta[..., :, None] * jnp.einsum('bhntk,bhnsk->bhnts', k, k) * decay, 0.0)
    eG = jnp.exp(G)
    rhs = jnp.concatenate([(beta * eG)[..., None] * k, beta[..., None] * v], axis=-1)
    sol = lax.linalg.triangular_solve(A, rhs, left_side=True, lower=True, unit_diagonal=True)
    W, U0 = sol[..., :dk], sol[..., dk:]
    P = jnp.einsum('bhntk,bhnsk->bhnts', q, k) * decay
    kdec = jnp.exp(G[..., -1:] - G)[..., None] * k
    gL = G[..., -1]
    cf = lambda a: jnp.moveaxis(a, 2, 0)

    def step(S, inp):
        Wc, Uc, Pc, qc, eGc, kdc, gLc = inp
        U = Uc - jnp.einsum('bhlk,bhkv->bhlv', Wc, S)
        o = eGc[..., None] * jnp.einsum('bhlk,bhkv->bhlv', qc, S) + jnp.einsum('bhts,bhsv->bhtv', Pc, U)
        S = jnp.exp(gLc)[..., None, None] * S + jnp.einsum('bhlk,bhlv->bhkv', kdc, U)
        return S, o

    S, o = lax.scan(step, S0, (cf(W), cf(U0), cf(P), cf(q), cf(eG), cf(kdec), cf(gL)))
    return jnp.moveaxis(o, 0, 2).reshape(bsz, nh, t, -1), S


def gdn_mixer(h, w_in, w_conv, A_log, dt_bias, g_norm, w_out, rows, init):
    bsz, t, _ = h.shape
    f32 = jnp.float32
    qkv, z, ab = jnp.split(h @ w_in, (GD_CONV_CH, GD_CONV_CH + GD_INNER), axis=-1)
    qkv = jax.nn.silu(dwconv3(qkv, w_conv, rows))
    q, k, v = jnp.split(qkv, (GD_QK, 2 * GD_QK), axis=-1)

    def l2n(a):
        af = a.astype(f32)
        return af * lax.rsqrt(jnp.sum(af * af, axis=-1, keepdims=True) + EPS)

    rep = GD_V_HEADS // GD_QK_HEADS
    q = jnp.repeat(l2n(to_heads(q, GD_QK_HEADS)), rep, axis=1) * (GD_DK ** -0.5)
    k = jnp.repeat(l2n(to_heads(k, GD_QK_HEADS)), rep, axis=1)
    v = to_heads(v, GD_V_HEADS).astype(f32)
    ab = ab.astype(f32).reshape(bsz, t, 4, GD_V_HEADS).transpose(2, 0, 3, 1)
    A = jnp.exp(A_log.astype(f32))
    dtb = dt_bias.astype(f32)
    g_f = -A[0][:, None] * jax.nn.softplus(ab[0] + dtb[0][:, None])
    g_b = -A[1][:, None] * jax.nn.softplus(ab[1] + dtb[1][:, None])
    beta_f, beta_b = jax.nn.sigmoid(ab[2]), jax.nn.sigmoid(ab[3])
    o_f, S_f = gdn_scan(q, k, v, beta_f, g_f, init[:, 0])
    o_b, S_b = gdn_scan(flip_t(q), flip_t(k), flip_t(v), flip_t(beta_b), flip_t(g_b), init[:, 1])
    o = (o_f + flip_t(o_b)).transpose(0, 2, 1, 3)
    o = rmsnorm(o, g_norm).reshape(bsz, t, GD_INNER).astype(h.dtype) * jax.nn.silu(z)
    return o @ w_out, (S_f, S_b)


def setup_inputs(seed: int = 0) -> dict:
    key = jax.random.key(seed)
    ks = jax.random.split(key, 32)
    f32 = jnp.float32
    nrm = lambda k, shape, s: jax.random.normal(k, shape, f32) * s
    H = ML_HEADS
    lin = jnp.linspace(3.0, 6.0, H, dtype=f32)
    f_bias = jnp.broadcast_to(jnp.concatenate([lin, lin]), (N_ML, 2 * H))
    b_ml_gate = jnp.concatenate([nrm(ks[0], (N_ML, 2 * H), 0.1), f_bias + nrm(ks[1], (N_ML, 2 * H), 0.1)], axis=-1)
    dt = jnp.exp(jax.random.uniform(ks[2], (N_GD, 2, GD_V_HEADS), f32, math.log(1e-3), math.log(1e-1)))
    return {
        "x_prompt": nrm(ks[3], (BATCH, SEQ, D_MODEL), 1.0),
        "x_sample": nrm(ks[4], (DEC_BATCH, DEC_SEQ, D_MODEL), 1.0),
        "c": nrm(ks[5], (DEC_BATCH, D_MODEL), 1.0),
        "cache_ml_C": nrm(ks[6], (DEC_BATCH, N_ML, 2, ML_HEADS, ML_DK, ML_DV), 0.05),
        "cache_ml_n": nrm(ks[7], (DEC_BATCH, N_ML, 2, ML_HEADS, ML_DK), 0.05),
        "cache_ml_m": nrm(ks[8], (DEC_BATCH, N_ML, 2, ML_HEADS), 0.5),
        "cache_gd_S": nrm(ks[9], (DEC_BATCH, N_GD, 2, GD_V_HEADS, GD_DK, GD_DV), 0.05),
        "c_ctx": nrm(ks[10], (D_MODEL,), 1.0),
        "w_ada": nrm(ks[11], (DEPTH, D_MODEL, 3 * D_MODEL), 0.5 * D_MODEL ** -0.5),
        "b_ada": nrm(ks[12], (DEPTH, 3 * D_MODEL), 0.01),
        "g_norm": 1.0 + nrm(ks[13], (DEPTH, D_MODEL), 0.02),
        "w_ml_in": nrm(ks[14], (N_ML, D_MODEL, ML_IN_COLS), D_MODEL ** -0.5),
        "b_ml_gate": b_ml_gate,
        "g_ml_head": 1.0 + nrm(ks[15], (N_ML, ML_INNER), 0.02),
        "w_ml_out": nrm(ks[16], (N_ML, ML_INNER, D_MODEL), ML_INNER ** -0.5),
        "w_sc_in": nrm(ks[17], (N_SC, D_MODEL, SC_IN_COLS), D_MODEL ** -0.5),
        "w_sc_conv": nrm(ks[18], (N_SC, CONV_W, SC_INNER), CONV_W ** -0.5),
        "w_sc_out": nrm(ks[19], (N_SC, SC_INNER, D_MODEL), SC_INNER ** -0.5),
        "w_gd_in": nrm(ks[20], (N_GD, D_MODEL, GD_IN_COLS), D_MODEL ** -0.5),
        "w_gd_conv": nrm(ks[21], (N_GD, CONV_W, GD_CONV_CH), CONV_W ** -0.5),
        "gd_A_log": jnp.log(jax.random.uniform(ks[22], (N_GD, 2, GD_V_HEADS), f32, 1.0, 16.0)),
        "gd_dt_bias": dt + jnp.log(-jnp.expm1(-dt)),
        "g_gd_norm": 1.0 + nrm(ks[23], (N_GD, GD_DV), 0.02),
        "w_gd_out": nrm(ks[24], (N_GD, GD_INNER, D_MODEL), GD_INNER ** -0.5),
        "g_final": 1.0 + nrm(ks[25], (D_MODEL,), 0.02),
    }


def reference(x_prompt, x_sample, c, cache_ml_C, cache_ml_n, cache_ml_m, cache_gd_S, c_ctx,
              w_ada, b_ada, g_norm, w_ml_in, b_ml_gate, g_ml_head, w_ml_out,
              w_sc_in, w_sc_conv, w_sc_out, w_gd_in, w_gd_conv, gd_A_log, gd_dt_bias, g_gd_norm, w_gd_out,
              g_final):
    f32 = jnp.float32

    def stream(x, cond, rows, ml_init, gd_init, collect):
        ml_states, gd_states = [], []
        for l in range(DEPTH):
            mod = jax.nn.silu(cond) @ w_ada[l] + b_ada[l]
            shift, scale, gate = jnp.split(mod[:, None, :], 3, axis=-1)
            h = rmsnorm(x, g_norm[l]) * (1.0 + scale) + shift
            j = l // N_MIXERS
            kind = l % N_MIXERS
            if kind == 0:
                out, st = mlstm_mixer(h, w_ml_in[j], b_ml_gate[j], g_ml_head[j], w_ml_out[j], ml_init(j))
                if collect:
                    ml_states.append(st)
            elif kind == 1:
                out = shortconv_mixer(h, w_sc_in[j], w_sc_conv[j], w_sc_out[j], rows)
            else:
                out, st = gdn_mixer(h, w_gd_in[j], w_gd_conv[j], gd_A_log[j], gd_dt_bias[j], g_gd_norm[j],
                                    w_gd_out[j], rows, gd_init(j))
                if collect:
                    gd_states.append(st)
            x = x + gate * out
        return rmsnorm(x, g_final), ml_states, gd_states

    bp = x_prompt.shape[0]
    ml_zero = lambda j: (jnp.zeros((bp, 2, ML_HEADS, ML_DK, ML_DV), f32),
                         jnp.zeros((bp, 2, ML_HEADS, ML_DK), f32),
                         jnp.zeros((bp, 2, ML_HEADS), f32))
    gd_zero = lambda j: jnp.zeros((bp, 2, GD_V_HEADS, GD_DK, GD_DV), f32)
    y_prompt, ml_st, gd_st = stream(x_prompt, c_ctx[None, :], 1, ml_zero, gd_zero, True)

    rows = x_sample.shape[1] // GRID_W
    ml_cache = lambda j: (cache_ml_C[:, j].astype(f32), cache_ml_n[:, j].astype(f32), cache_ml_m[:, j].astype(f32))
    gd_cache = lambda j: cache_gd_S[:, j].astype(f32)
    y_sample, _, _ = stream(x_sample, c, rows, ml_cache, gd_cache, False)

    dt = x_prompt.dtype
    state_ml_C = jnp.stack([jnp.stack(s[0], axis=1) for s in ml_st], axis=1).astype(dt)
    state_ml_n = jnp.stack([jnp.stack(s[1], axis=1) for s in ml_st], axis=1).astype(dt)
    state_ml_m = jnp.stack([jnp.stack(s[2], axis=1) for s in ml_st], axis=1).astype(dt)
    state_gd_S = jnp.stack([jnp.stack(s, axis=1) for s in gd_st], axis=1).astype(dt)
    return (y_prompt, y_sample, state_ml_C, state_ml_n, state_ml_m, state_gd_S)
```

```python
import functools

import jax
import jax.numpy as jnp
from jax import lax
from jax.experimental import pallas as pl
from jax.experimental.pallas import tpu as pltpu

F32 = jnp.float32
BF16 = jnp.bfloat16
HIGHEST = lax.Precision.HIGHEST

D_MODEL = 2048
BATCH = 16
SEQ = 256
DEPTH = 4
DEC_BATCH = 2
DEC_SEQ = 2048
GRID_W = 64
EPS = 1e-6

T_CTX = BATCH * SEQ
T_LAT = DEC_BATCH * DEC_SEQ
T_ALL = T_CTX + T_LAT
N_COND = 8

ML_HEADS = 8
ML_DK = 256
ML_DV = 512
ML_QK = ML_HEADS * ML_DK
ML_INNER = ML_HEADS * ML_DV
ML_MAIN = 2 * ML_QK + 3 * ML_INNER

SC_INNER = 2 * D_MODEL

GD_DK = 128
GD_DV = 128
GD_QK_HEADS = 16
GD_V_HEADS = 32
GD_QK = GD_QK_HEADS * GD_DK
GD_INNER = GD_V_HEADS * GD_DV
GD_CONV_CH = 2 * GD_QK + GD_INNER
GD_MAIN = GD_CONV_CH + GD_INNER
GD_GROUPS = 4
GD_GV = GD_V_HEADS // GD_GROUPS
GD_GQ = GD_QK_HEADS // GD_GROUPS

CHUNK = 64
N_SLOT = T_ALL // CHUNK
CTX_SLOTS = T_CTX // CHUNK
CTX_CHUNKS = SEQ // CHUNK
LAT_CHUNKS = DEC_SEQ // CHUNK

LANE = 128
VMEM_LIMIT = 56 * 1024 * 1024


def _cparams(n_axes):
    return pltpu.CompilerParams(dimension_semantics=("arbitrary",) * n_axes, vmem_limit_bytes=VMEM_LIMIT)


def _sigmoid(x):
    return 1.0 / (1.0 + jnp.exp(-x))


def _silu(x):
    return x * _sigmoid(x)


def _softplus(x):
    return jnp.maximum(x, 0.0) + jnp.log1p(jnp.exp(-jnp.abs(x)))


def _log_sigmoid(x):
    return -_softplus(-x)


def _cond_group(row0):
    return jnp.where(row0 < T_CTX, 0, 1 + (row0 - T_CTX) // DEC_SEQ)


def _conv_row_len(row0):
    return jnp.where(row0 < T_CTX, SEQ, GRID_W)


def _dot_nt(a, b, **kw):
    return lax.dot_general(a, b, (((1,), (1,)), ((), ())), preferred_element_type=F32, **kw)


def _dot_tn(a, b, **kw):
    return lax.dot_general(a, b, (((0,), (0,)), ((), ())), preferred_element_type=F32, **kw)


def _dot(a, b, **kw):
    return jnp.dot(a, b, preferred_element_type=F32, **kw)


def _mod_kernel(s_ref, w_ref, b_ref, o_ref):
    a = _silu(s_ref[...]).astype(BF16)
    o_ref[...] = _dot(a, w_ref[...].astype(BF16)) + b_ref[...]


def _modulation(cond, w_ada, b_ada):
    tn = 1024
    n = 3 * D_MODEL
    return pl.pallas_call(
        _mod_kernel,
        grid=(DEPTH, n // tn),
        in_specs=[
            pl.BlockSpec((N_COND, D_MODEL), lambda l, j: (0, 0)),
            pl.BlockSpec((None, D_MODEL, tn), lambda l, j: (l, 0, j)),
            pl.BlockSpec((None, 1, tn), lambda l, j: (l, 0, j)),
        ],
        out_specs=pl.BlockSpec((None, N_COND, tn), lambda l, j: (l, 0, j)),
        out_shape=jax.ShapeDtypeStruct((DEPTH, N_COND, n), F32),
        compiler_params=_cparams(2),
        name="adaln_modulation",
    )(cond, w_ada, b_ada.reshape(DEPTH, 1, n))


def _norm_mod_kernel(x_ref, g_ref, shift_ref, scale_ref, o_ref, *, tm):
    grp = _cond_group(pl.program_id(0) * tm)
    x = x_ref[...]
    r = lax.rsqrt(jnp.mean(x * x, axis=-1, keepdims=True) + EPS)
    xn = x * r * g_ref[...]
    sh = shift_ref[pl.ds(grp, 1), :]
    sc = scale_ref[pl.ds(grp, 1), :]
    o_ref[...] = (xn * (1.0 + sc) + sh).astype(BF16)


def _norm_mod(x, g, shift, scale):
    tm = 512
    return pl.pallas_call(
        functools.partial(_norm_mod_kernel, tm=tm),
        grid=(T_ALL // tm,),
        in_specs=[
            pl.BlockSpec((tm, D_MODEL), lambda i: (i, 0)),
            pl.BlockSpec((1, D_MODEL), lambda i: (0, 0)),
            pl.BlockSpec((N_COND, D_MODEL), lambda i: (0, 0)),
            pl.BlockSpec((N_COND, D_MODEL), lambda i: (0, 0)),
        ],
        out_specs=pl.BlockSpec((tm, D_MODEL), lambda i: (i, 0)),
        out_shape=jax.ShapeDtypeStruct((T_ALL, D_MODEL), BF16),
        compiler_params=_cparams(1),
        name="norm_modulate",
    )(x, g.reshape(1, D_MODEL), shift, scale)


def _final_norm_kernel(x_ref, g_ref, o_ref):
    x = x_ref[...]
    r = lax.rsqrt(jnp.mean(x * x, axis=-1, keepdims=True) + EPS)
    o_ref[...] = x * r * g_ref[...]


def _final_norm(x, g):
    tm = 512
    return pl.pallas_call(
        _final_norm_kernel,
        grid=(T_ALL // tm,),
        in_specs=[
            pl.BlockSpec((tm, D_MODEL), lambda i: (i, 0)),
            pl.BlockSpec((1, D_MODEL), lambda i: (0, 0)),
        ],
        out_specs=pl.BlockSpec((tm, D_MODEL), lambda i: (i, 0)),
        out_shape=jax.ShapeDtypeStruct((T_ALL, D_MODEL), F32),
        compiler_params=_cparams(1),
        name="final_norm",
    )(x, g.reshape(1, D_MODEL))


def _proj_kernel(a_ref, w_ref, o_ref, wb_ref):
    @pl.when(pl.program_id(1) == 0)
    def _():
        wb_ref[...] = w_ref[...].astype(BF16)

    o_ref[...] = _dot(a_ref[...], wb_ref[...])


def _project(a, w, layer, n_cols, tn, tm=512):
    k = a.shape[1]
    return pl.pallas_call(
        _proj_kernel,
        grid=(n_cols // tn, T_ALL // tm),
        in_specs=[
            pl.BlockSpec((tm, k), lambda j, i: (i, 0)),
            pl.BlockSpec((None, k, tn), lambda j, i: (layer, 0, j)),
        ],
        out_specs=pl.BlockSpec((tm, tn), lambda j, i: (i, j)),
        out_shape=jax.ShapeDtypeStruct((T_ALL, n_cols), F32),
        scratch_shapes=[pltpu.VMEM((k, tn), BF16)],
        compiler_params=_cparams(2),
        name="in_projection",
    )(a, w)


def _proj_res_kernel(a_ref, w_ref, x_ref, gate_ref, o_ref, wb_ref, *, tm):
    i = pl.program_id(1)

    @pl.when(i == 0)
    def _():
        wb_ref[...] = w_ref[...].astype(BF16)

    gate = gate_ref[pl.ds(_cond_group(i * tm), 1), :]
    o_ref[...] = x_ref[...] + gate * _dot(a_ref[...], wb_ref[...])


def _project_residual(a, w, layer, x, gate):
    k = a.shape[1]
    tm, tn = 512, 512
    return pl.pallas_call(
        functools.partial(_proj_res_kernel, tm=tm),
        grid=(D_MODEL // tn, T_ALL // tm),
        in_specs=[
            pl.BlockSpec((tm, k), lambda j, i: (i, 0)),
            pl.BlockSpec((None, k, tn), lambda j, i: (layer, 0, j)),
            pl.BlockSpec((tm, tn), lambda j, i: (i, j)),
            pl.BlockSpec((N_COND, tn), lambda j, i: (0, j)),
        ],
        out_specs=pl.BlockSpec((tm, tn), lambda j, i: (i, j)),
        out_shape=jax.ShapeDtypeStruct((T_ALL, D_MODEL), F32),
        scratch_shapes=[pltpu.VMEM((k, tn), BF16)],
        compiler_params=_cparams(2),
        name="out_projection_residual",
    )(a, w, x, gate)


def _slot_of(d, g):
    return g + d * (N_SLOT - 1 - 2 * g)


def _seq_of(j):
    return jnp.where(j < CTX_SLOTS, j // CTX_CHUNKS, BATCH + (j - CTX_SLOTS) // LAT_CHUNKS)


def _slot_flags(d, j):
    is_ctx = j < CTX_SLOTS
    pos = jnp.where(is_ctx, j % CTX_CHUNKS, (j - CTX_SLOTS) % LAT_CHUNKS)
    n_chunks = jnp.where(is_ctx, CTX_CHUNKS, LAT_CHUNKS)
    head_pos = jnp.where(d == 0, 0, n_chunks - 1)
    tail_pos = jnp.where(d == 0, n_chunks - 1, 0)
    return is_ctx, pos == head_pos, pos == tail_pos


def _scan_mask(d):
    t_i = lax.broadcasted_iota(jnp.int32, (CHUNK, CHUNK), 0)
    s_i = lax.broadcasted_iota(jnp.int32, (CHUNK, CHUNK), 1)
    diff = (t_i - s_i) * (1 - 2 * d)
    return diff >= 0, diff > 0


def _lat_batch(d, g):
    return jnp.clip(_seq_of(_slot_of(d, g)) - BATCH, 0, DEC_BATCH - 1)


def _ctx_seq(d, g):
    return jnp.minimum(_seq_of(_slot_of(d, g)), BATCH - 1)


def _mlstm_kernel(q_ref, k_ref, v_ref, gc_ref, gr_ref, bc_ref, br_ref, c0_ref, n0_ref, m0_ref,
                  h_ref, cout_ref, nout_ref, mout_ref, c_s, n_s, m_s):
    d = pl.program_id(0)
    j = _slot_of(d, pl.program_id(1))
    is_ctx, is_head, is_tail = _slot_flags(d, j)

    @pl.when(jnp.logical_and(is_head, is_ctx))
    def _():
        c_s[...] = jnp.zeros_like(c_s)
        n_s[...] = jnp.zeros_like(n_s)
        m_s[...] = jnp.zeros_like(m_s)

    @pl.when(jnp.logical_and(is_head, jnp.logical_not(is_ctx)))
    def _():
        c_s[...] = c0_ref[...]
        n_s[...] = n0_ref[...]
        m_s[...] = m0_ref[...]

    mask, _ = _scan_mask(d)
    tri = mask.astype(F32)
    gc = gc_ref[...] + bc_ref[...]
    gr = gr_ref[...] + br_ref[...]
    i_c = gc[:, 0:ML_HEADS]
    b_c = _dot(tri, _log_sigmoid(gc[:, ML_HEADS:2 * ML_HEADS]), precision=HIGHEST)
    i_r = gr[0:ML_HEADS, :]
    b_r = _dot_nt(_log_sigmoid(gr[ML_HEADS:2 * ML_HEADS, :]), tri, precision=HIGHEST)
    bl_all = jnp.where(d == 0, b_c[CHUNK - 1:CHUNK, :], b_c[0:1, :])

    for h in range(ML_HEADS):
        q32 = q_ref[:, h * ML_DK:(h + 1) * ML_DK] * (ML_DK ** -0.5)
        k32 = k_ref[:, h * ML_DK:(h + 1) * ML_DK]
        qb = q32.astype(BF16)
        kb = k32.astype(BF16)
        vb = v_ref[:, h * ML_DV:(h + 1) * ML_DV].astype(BF16)
        bcol = b_c[:, h:h + 1]
        icol = i_c[:, h:h + 1]
        brow = b_r[h:h + 1, :]
        irow = i_r[h:h + 1, :]
        m = m_s[h:h + 1, 0:1]
        c = c_s[h]
        n = n_s[h:h + 1, :]

        dmat = jnp.where(mask, bcol - brow + irow, -jnp.inf)
        inter = bcol + m
        mt = jnp.maximum(inter, jnp.max(dmat, axis=-1, keepdims=True))
        s = _dot_nt(qb, kb) * jnp.exp(dmat - mt)
        sc = jnp.exp(inter - mt)
        num = sc * _dot(qb, c.astype(BF16)) + _dot(s.astype(BF16), vb)
        den = sc * jnp.sum(q32 * n, axis=-1, keepdims=True) + jnp.sum(s, axis=-1, keepdims=True)
        h_ref[:, h * ML_DV:(h + 1) * ML_DV] = num / jnp.maximum(jnp.abs(den), jnp.exp(-mt))

        bl = bl_all[:, h:h + 1]
        a = bl - bcol + icol
        m_new = jnp.maximum(bl + m, jnp.max(a, axis=0, keepdims=True))
        wk = jnp.exp(a - m_new) * k32
        dec = jnp.exp(bl + m - m_new)
        c_s[h] = dec * c + _dot_tn(wk.astype(BF16), vb)
        n_s[h:h + 1, :] = dec * n + jnp.sum(wk, axis=0, keepdims=True)
        m_s[h:h + 1, :] = jnp.broadcast_to(m_new, (1, LANE))

    @pl.when(jnp.logical_and(is_tail, is_ctx))
    def _():
        cout_ref[...] = c_s[...]
        nout_ref[...] = n_s[...]
        mout_ref[...] = m_s[...]


def _mlstm_scan(proj, gates, b_gate, cache_c, cache_n, cache_m, j_layer):
    H = ML_HEADS
    gd = jnp.stack([jnp.concatenate([gates[:, 0:H], gates[:, 2 * H:3 * H]], axis=1),
                    jnp.concatenate([gates[:, H:2 * H], gates[:, 3 * H:4 * H]], axis=1)], axis=0)
    gcol = gd.reshape(2, N_SLOT, CHUNK, 2 * H)
    grow = gcol.transpose(0, 1, 3, 2)
    bd = jnp.stack([jnp.concatenate([b_gate[0:H], b_gate[2 * H:3 * H]]),
                    jnp.concatenate([b_gate[H:2 * H], b_gate[3 * H:4 * H]])], axis=0)
    bcol = bd.reshape(2, 1, 2 * H)
    brow = bd.reshape(2, 2 * H, 1)
    m0 = jnp.broadcast_to(cache_m[:, j_layer][..., None], (DEC_BATCH, 2, H, LANE))

    return pl.pallas_call(
        _mlstm_kernel,
        grid=(2, N_SLOT),
        in_specs=[
            pl.BlockSpec((CHUNK, ML_QK), lambda d, g: (_slot_of(d, g), 0)),
            pl.BlockSpec((CHUNK, ML_QK), lambda d, g: (_slot_of(d, g), 1)),
            pl.BlockSpec((CHUNK, ML_INNER), lambda d, g: (_slot_of(d, g), 1)),
            pl.BlockSpec((None, None, CHUNK, 2 * H), lambda d, g: (d, _slot_of(d, g), 0, 0)),
            pl.BlockSpec((None, None, 2 * H, CHUNK), lambda d, g: (d, _slot_of(d, g), 0, 0)),
            pl.BlockSpec((None, 1, 2 * H), lambda d, g: (d, 0, 0)),
            pl.BlockSpec((None, 2 * H, 1), lambda d, g: (d, 0, 0)),
            pl.BlockSpec((None, None, None, H, ML_DK, ML_DV),
                         lambda d, g: (_lat_batch(d, g), j_layer, d, 0, 0, 0)),
            pl.BlockSpec((None, None, None, H, ML_DK), lambda d, g: (_lat_batch(d, g), j_layer, d, 0, 0)),
            pl.BlockSpec((None, None, H, LANE), lambda d, g: (_lat_batch(d, g), d, 0, 0)),
        ],
        out_specs=[
            pl.BlockSpec((None, CHUNK, ML_INNER), lambda d, g: (d, _slot_of(d, g), 0)),
            pl.BlockSpec((None, None, H, ML_DK, ML_DV), lambda d, g: (_ctx_seq(d, g), d, 0, 0, 0)),
            pl.BlockSpec((None, None, H, ML_DK), lambda d, g: (_ctx_seq(d, g), d, 0, 0)),
            pl.BlockSpec((None, None, H, LANE), lambda d, g: (_ctx_seq(d, g), d, 0, 0)),
        ],
        out_shape=[
            jax.ShapeDtypeStruct((2, T_ALL, ML_INNER), F32),
            jax.ShapeDtypeStruct((BATCH, 2, H, ML_DK, ML_DV), F32),
            jax.ShapeDtypeStruct((BATCH, 2, H, ML_DK), F32),
            jax.ShapeDtypeStruct((BATCH, 2, H, LANE), F32),
        ],
        scratch_shapes=[
            pltpu.VMEM((H, ML_DK, ML_DV), F32),
            pltpu.VMEM((H, ML_DK), F32),
            pltpu.VMEM((H, LANE), F32),
        ],
        compiler_params=_cparams(2),
        name="mlstm_scan",
    )(proj, proj, proj, gcol, grow, bcol, brow, cache_c, cache_n, m0)


def _mlstm_post_kernel(hf_ref, hb_ref, o_ref, z_ref, g_ref, y_ref):
    hs = hf_ref[...] + hb_ref[...]
    r = lax.rsqrt(jnp.mean(hs * hs, axis=-1, keepdims=True) + EPS)
    y = (hs * r * g_ref[...]) * _sigmoid(o_ref[...]) * _silu(z_ref[...])
    y_ref[...] = y.astype(BF16)


def _mlstm_post(hdir, proj, g_head):
    tm = 256
    o_blk = (2 * ML_QK + ML_INNER) // ML_DV
    z_blk = (2 * ML_QK + 2 * ML_INNER) // ML_DV
    return pl.pallas_call(
        _mlstm_post_kernel,
        grid=(T_ALL // tm, ML_HEADS),
        in_specs=[
            pl.BlockSpec((None, tm, ML_DV), lambda i, h: (0, i, h)),
            pl.BlockSpec((None, tm, ML_DV), lambda i, h: (1, i, h)),
            pl.BlockSpec((tm, ML_DV), lambda i, h: (i, o_blk + h)),
            pl.BlockSpec((tm, ML_DV), lambda i, h: (i, z_blk + h)),
            pl.BlockSpec((1, ML_DV), lambda i, h: (0, h)),
        ],
        out_specs=pl.BlockSpec((tm, ML_DV), lambda i, h: (i, h)),
        out_shape=jax.ShapeDtypeStruct((T_ALL, ML_INNER), BF16),
        compiler_params=_cparams(2),
        name="mlstm_gate_norm",
    )(hdir, hdir, proj, proj, g_head.reshape(1, ML_INNER))


def _dwconv3_tile(x, w_ref, row_len, tm):
    r = lax.broadcasted_iota(jnp.int32, (tm, 1), 0) & (row_len - 1)
    prev = jnp.where(r == 0, 0.0, pltpu.roll(x, 1, axis=0))
    nxt = jnp.where(r == row_len - 1, 0.0, pltpu.roll(x, tm - 1, axis=0))
    return prev * w_ref[0:1, :] + x * w_ref[1:2, :] + nxt * w_ref[2:3, :]


def _sc_post_kernel(u_ref, b_ref, c_ref, z_ref, w_ref, y_ref, *, tm):
    row_len = _conv_row_len(pl.program_id(0) * tm)
    conv = _dwconv3_tile(c_ref[...] * u_ref[...], w_ref, row_len, tm)
    y_ref[...] = (b_ref[...] * conv * _silu(z_ref[...])).astype(BF16)


def _sc_post(proj, w_conv):
    tm, tn = 256, 512
    nb = SC_INNER // tn
    return pl.pallas_call(
        functools.partial(_sc_post_kernel, tm=tm),
        grid=(T_ALL // tm, nb),
        in_specs=[
            pl.BlockSpec((tm, tn), lambda i, j: (i, j)),
            pl.BlockSpec((tm, tn), lambda i, j: (i, nb + j)),
            pl.BlockSpec((tm, tn), lambda i, j: (i, 2 * nb + j)),
            pl.BlockSpec((tm, tn), lambda i, j: (i, 3 * nb + j)),
            pl.BlockSpec((3, tn), lambda i, j: (0, j)),
        ],
        out_specs=pl.BlockSpec((tm, tn), lambda i, j: (i, j)),
        out_shape=jax.ShapeDtypeStruct((T_ALL, SC_INNER), BF16),
        compiler_params=_cparams(2),
        name="shortconv_gate",
    )(proj, proj, proj, proj, w_conv)


def _gdn_prep_kernel(x_ref, w_ref, o_ref, *, tm, tn):
    j = pl.program_id(1)
    row_len = _conv_row_len(pl.program_id(0) * tm)
    y = _silu(_dwconv3_tile(x_ref[...], w_ref, row_len, tm))
    q_blocks = GD_QK // tn
    is_qk = j < 2 * q_blocks
    q_scale = jnp.where(j < q_blocks, GD_DK ** -0.5, 1.0)
    for hh in range(tn // GD_DK):
        yh = y[:, hh * GD_DK:(hh + 1) * GD_DK]
        r = lax.rsqrt(jnp.sum(yh * yh, axis=-1, keepdims=True) + EPS) * q_scale
        o_ref[:, hh * GD_DK:(hh + 1) * GD_DK] = yh * jnp.where(is_qk, r, 1.0)


def _gdn_prep(proj, w_conv):
    tm, tn = 256, 512
    return pl.pallas_call(
        functools.partial(_gdn_prep_kernel, tm=tm, tn=tn),
        grid=(T_ALL // tm, GD_CONV_CH // tn),
        in_specs=[
            pl.BlockSpec((tm, tn), lambda i, j: (i, j)),
            pl.BlockSpec((3, tn), lambda i, j: (0, j)),
        ],
        out_specs=pl.BlockSpec((tm, tn), lambda i, j: (i, j)),
        out_shape=jax.ShapeDtypeStruct((T_ALL, GD_CONV_CH), F32),
        compiler_params=_cparams(2),
        name="gdn_conv_norm",
    )(proj, w_conv)


def _gdn_kernel(q_ref, k_ref, v_ref, gc_ref, gr_ref, pc_ref, pr_ref, s0_ref,
                o_ref, sout_ref, s_s):
    d = pl.program_id(0)
    j = _slot_of(d, pl.program_id(2))
    is_ctx, is_head, is_tail = _slot_flags(d, j)

    @pl.when(jnp.logical_and(is_head, is_ctx))
    def _():
        s_s[...] = jnp.zeros_like(s_s)

    @pl.when(jnp.logical_and(is_head, jnp.logical_not(is_ctx)))
    def _():
        s_s[...] = s0_ref[...]

    incl, strict = _scan_mask(d)
    tri = incl.astype(F32)
    nh = GD_GV
    gc = gc_ref[...]
    gr = gr_ref[...]
    g_c = -jnp.exp(pc_ref[0:1, :]) * _softplus(gc[:, 0:nh] + pc_ref[1:2, :])
    G_c = _dot(tri, g_c, precision=HIGHEST)
    beta_c = _sigmoid(gc[:, nh:2 * nh])
    g_r = -jnp.exp(pr_ref[:, 0:1]) * _softplus(gr[0:nh, :] + pr_ref[:, 1:2])
    G_r = _dot_nt(g_r, tri, precision=HIGHEST)
    Gl_all = jnp.where(d == 0, G_c[CHUNK - 1:CHUNK, :], G_c[0:1, :])

    for hq in range(GD_GQ):
        q32 = q_ref[:, hq * GD_DK:(hq + 1) * GD_DK]
        k32 = k_ref[:, hq * GD_DK:(hq + 1) * GD_DK]
        qb = q32.astype(BF16)
        kb = k32.astype(BF16)
        kk = _dot_nt(kb, kb)
        qk = _dot_nt(qb, kb)
        for r in range(GD_V_HEADS // GD_QK_HEADS):
            hv = hq * (GD_V_HEADS // GD_QK_HEADS) + r
            v32 = v_ref[:, hv * GD_DV:(hv + 1) * GD_DV]
            Gc = G_c[:, hv:hv + 1]
            Gr = G_r[hv:hv + 1, :]
            beta = beta_c[:, hv:hv + 1]
            Gl = Gl_all[:, hv:hv + 1]
            S = s_s[hv]
            Sb = S.astype(BF16)

            decay = jnp.where(incl, jnp.exp(jnp.where(incl, Gc - Gr, 0.0)), 0.0)
            neg_a = jnp.where(strict, -(beta * kk * decay), 0.0)
            eG = jnp.exp(Gc)
            x = jnp.concatenate([(beta * eG) * k32, beta * v32], axis=-1)
            pw = neg_a
            for step in range(6):
                x = x + _dot(pw, x, precision=HIGHEST)
                if step < 5:
                    pw = _dot(pw, pw, precision=HIGHEST)
            w_mat = x[:, 0:GD_DK]
            u0 = x[:, GD_DK:GD_DK + GD_DV]

            u = u0 - _dot(w_mat.astype(BF16), Sb)
            ub = u.astype(BF16)
            p_mat = (qk * decay).astype(BF16)
            o_ref[:, hv * GD_DV:(hv + 1) * GD_DV] = eG * _dot(qb, Sb) + _dot(p_mat, ub)
            kdec = (jnp.exp(Gl - Gc) * k32).astype(BF16)
            s_s[hv] = jnp.exp(Gl) * S + _dot_tn(kdec, ub)

    @pl.when(jnp.logical_and(is_tail, is_ctx))
    def _():
        sout_ref[...] = s_s[...]


def _gdn_scan(qkv, ab, a_log, dt_bias, cache_s, j_layer):
    HV, nh = GD_V_HEADS, GD_GV
    a_d = ab[:, 0:2 * HV].reshape(T_ALL, 2, GD_GROUPS, nh)
    b_d = ab[:, 2 * HV:4 * HV].reshape(T_ALL, 2, GD_GROUPS, nh)
    gcol = jnp.concatenate([a_d, b_d], axis=-1).reshape(N_SLOT, CHUNK, 2, GD_GROUPS, 2 * nh)
    gcol = gcol.transpose(2, 3, 0, 1, 4)
    grow = gcol.transpose(0, 1, 2, 4, 3)
    par = jnp.stack([a_log.astype(F32), dt_bias.astype(F32)], axis=1)
    pcol = par.reshape(2, 2, GD_GROUPS, nh).transpose(0, 2, 1, 3)
    prow = pcol.transpose(0, 1, 3, 2)

    qb = GD_GQ * GD_DK
    vb = nh * GD_DV
    return pl.pallas_call(
        _gdn_kernel,
        grid=(2, GD_GROUPS, N_SLOT),
        in_specs=[
            pl.BlockSpec((CHUNK, qb), lambda d, p, g: (_slot_of(d, g), p)),
            pl.BlockSpec((CHUNK, qb), lambda d, p, g: (_slot_of(d, g), GD_GROUPS + p)),
            pl.BlockSpec((CHUNK, vb), lambda d, p, g: (_slot_of(d, g), GD_GROUPS + p)),
            pl.BlockSpec((None, None, None, CHUNK, 2 * nh), lambda d, p, g: (d, p, _slot_of(d, g), 0, 0)),
            pl.BlockSpec((None, None, None, 2 * nh, CHUNK), lambda d, p, g: (d, p, _slot_of(d, g), 0, 0)),
            pl.BlockSpec((None, None, 2, nh), lambda d, p, g: (d, p, 0, 0)),
            pl.BlockSpec((None, None, nh, 2), lambda d, p, g: (d, p, 0, 0)),
            pl.BlockSpec((None, None, None, nh, GD_DK, GD_DV),
                         lambda d, p, g: (_lat_batch(d, g), j_layer, d, p, 0, 0)),
        ],
        out_specs=[
            pl.BlockSpec((None, CHUNK, vb), lambda d, p, g: (d, _slot_of(d, g), p)),
            pl.BlockSpec((None, None, nh, GD_DK, GD_DV), lambda d, p, g: (_ctx_seq(d, g), d, p, 0, 0)),
        ],
        out_shape=[
            jax.ShapeDtypeStruct((2, T_ALL, GD_INNER), F32),
            jax.ShapeDtypeStruct((BATCH, 2, HV, GD_DK, GD_DV), F32),
        ],
        scratch_shapes=[pltpu.VMEM((nh, GD_DK, GD_DV), F32)],
        compiler_params=_cparams(3),
        name="gdn_scan",
    )(qkv, qkv, qkv, gcol, grow, pcol, prow, cache_s)


def _gdn_post_kernel(of_ref, ob_ref, z_ref, g_ref, y_ref, *, tn):
    g = g_ref[...]
    for hh in range(tn // GD_DV):
        sl = slice(hh * GD_DV, (hh + 1) * GD_DV)
        o = of_ref[:, sl] + ob_ref[:, sl]
        r = lax.rsqrt(jnp.mean(o * o, axis=-1, keepdims=True) + EPS)
        y_ref[:, sl] = ((o * r * g) * _silu(z_ref[:, sl])).astype(BF16)


def _gdn_post(odir, proj, g_norm):
    tm, tn = 256, 512
    z_blk = GD_CONV_CH // tn
    return pl.pallas_call(
        functools.partial(_gdn_post_kernel, tn=tn),
        grid=(T_ALL // tm, GD_INNER // tn),
        in_specs=[
            pl.BlockSpec((None, tm, tn), lambda i, j: (0, i, j)),
            pl.BlockSpec((None, tm, tn), lambda i, j: (1, i, j)),
            pl.BlockSpec((tm, tn), lambda i, j: (i, z_blk + j)),
            pl.BlockSpec((1, GD_DV), lambda i, j: (0, 0)),
        ],
        out_specs=pl.BlockSpec((tm, tn), lambda i, j: (i, j)),
        out_shape=jax.ShapeDtypeStruct((T_ALL, GD_INNER), BF16),
        compiler_params=_cparams(2),
        name="gdn_gate_norm",
    )(odir, odir, proj, g_norm.reshape(1, GD_DV))


def _gate_columns(h, w_in, layer, n_main):
    wg = w_in[layer, :, n_main:]
    n_gate = wg.shape[1]
    wg = jnp.pad(wg, ((0, 0), (0, LANE - n_gate)))
    return _project(h, wg[None], 0, LANE, LANE)[:, :n_gate]


def kernel(x_prompt, x_sample, c, cache_ml_C, cache_ml_n, cache_ml_m, cache_gd_S, c_ctx, w_ada, b_ada, g_norm,
           w_ml_in, b_ml_gate, g_ml_head, w_ml_out, w_sc_in, w_sc_conv, w_sc_out, w_gd_in, w_gd_conv, gd_A_log,
           gd_dt_bias, g_gd_norm, w_gd_out, g_final):
    x = jnp.concatenate([x_prompt.reshape(T_CTX, D_MODEL), x_sample.reshape(T_LAT, D_MODEL)], axis=0)
    cond = jnp.concatenate([c_ctx[None, :], c, jnp.zeros((N_COND - 1 - DEC_BATCH, D_MODEL), F32)], axis=0)
    mod = _modulation(cond, w_ada, b_ada)

    ml_c, ml_n, ml_m, gd_s = [], [], [], []
    for l in range(DEPTH):
        shift = mod[l, :, 0:D_MODEL]
        scale = mod[l, :, D_MODEL:2 * D_MODEL]
        gate = mod[l, :, 2 * D_MODEL:3 * D_MODEL]
        h = _norm_mod(x, g_norm[l], shift, scale)
        j = l // 3
        kind = l % 3
        if kind == 0:
            proj = _project(h, w_ml_in, j, ML_MAIN, 1024)
            gates = _gate_columns(h, w_ml_in, j, ML_MAIN)
            hdir, c_fin, n_fin, m_fin = _mlstm_scan(proj, gates, b_ml_gate[j], cache_ml_C, cache_ml_n,
                                                    cache_ml_m, j)
            ml_c.append(c_fin)
            ml_n.append(n_fin)
            ml_m.append(m_fin[..., 0])
            y = _mlstm_post(hdir, proj, g_ml_head[j])
            x = _project_residual(y, w_ml_out, j, x, gate)
        elif kind == 1:
            proj = _project(h, w_sc_in, j, 4 * SC_INNER, 1024)
            y = _sc_post(proj, w_sc_conv[j])
            x = _project_residual(y, w_sc_out, j, x, gate)
        else:
            proj = _project(h, w_gd_in, j, GD_MAIN, 1024)
            ab = _gate_columns(h, w_gd_in, j, GD_MAIN)
            qkv = _gdn_prep(proj, w_gd_conv[j])
            odir, s_fin = _gdn_scan(qkv, ab, gd_A_log[j], gd_dt_bias[j], cache_gd_S, j)
            gd_s.append(s_fin)
            y = _gdn_post(odir, proj, g_gd_norm[j])
            x = _project_residual(y, w_gd_out, j, x, gate)

    y_all = _final_norm(x, g_final)
    y_prompt = y_all[:T_CTX].reshape(BATCH, SEQ, D_MODEL)
    y_sample = y_all[T_CTX:].reshape(DEC_BATCH, DEC_SEQ, D_MODEL)
    state_ml_c = jnp.stack(ml_c, axis=1)
    state_ml_n = jnp.stack(ml_n, axis=1)
    state_ml_m = jnp.stack(ml_m, axis=1)
    state_gd_s = jnp.stack(gd_s, axis=1)
    return (y_prompt, y_sample, state_ml_c, state_ml_n, state_ml_m, state_gd_s)
```

```python
import functools

import jax
import jax.numpy as jnp
from jax import lax
from jax.experimental import pallas as pl
from jax.experimental.pallas import tpu as pltpu

F32 = jnp.float32
BF16 = jnp.bfloat16
HIGHEST = lax.Precision.HIGHEST

D_MODEL = 2048
BATCH = 16
SEQ = 256
DEPTH = 4
DEC_BATCH = 2
DEC_SEQ = 2048
GRID_W = 64
EPS = 1e-6

T_CTX = BATCH * SEQ
T_LAT = DEC_BATCH * DEC_SEQ
T_ALL = T_CTX + T_LAT
N_COND = 8

ML_HEADS = 8
ML_DK = 256
ML_DV = 512
ML_QK = ML_HEADS * ML_DK
ML_INNER = ML_HEADS * ML_DV
ML_MAIN = 2 * ML_QK + 3 * ML_INNER
N_ML = 2
ML_CHUNK = 256
ML_CBLK = 256

SC_INNER = 2 * D_MODEL

GD_DK = 128
GD_DV = 128
GD_QK_HEADS = 16
GD_V_HEADS = 32
GD_QK = GD_QK_HEADS * GD_DK
GD_INNER = GD_V_HEADS * GD_DV
GD_CONV_CH = 2 * GD_QK + GD_INNER
GD_MAIN = GD_CONV_CH + GD_INNER
GD_GROUPS = 2
GD_GV = GD_V_HEADS // GD_GROUPS
GD_GQ = GD_QK_HEADS // GD_GROUPS
GD_CHUNK = 64

LANE = 128
VMEM_LIMIT = 56 * 1024 * 1024


def _cparams(n_axes):
    return pltpu.CompilerParams(dimension_semantics=("arbitrary",) * n_axes, vmem_limit_bytes=VMEM_LIMIT)


def _sigmoid(x):
    return 1.0 / (1.0 + jnp.exp(-x))


def _silu(x):
    return x * _sigmoid(x)


def _softplus(x):
    return jnp.maximum(x, 0.0) + jnp.log1p(jnp.exp(-jnp.abs(x)))


def _log_sigmoid(x):
    return -_softplus(-x)


def _cond_group(row0):
    return jnp.where(row0 < T_CTX, 0, 1 + (row0 - T_CTX) // DEC_SEQ)


def _conv_row_len(row0):
    return jnp.where(row0 < T_CTX, SEQ, GRID_W)


def _dot_nt(a, b, **kw):
    return lax.dot_general(a, b, (((1,), (1,)), ((), ())), preferred_element_type=F32, **kw)


def _dot_tn(a, b, **kw):
    return lax.dot_general(a, b, (((0,), (0,)), ((), ())), preferred_element_type=F32, **kw)


def _dot(a, b, **kw):
    return jnp.dot(a, b, preferred_element_type=F32, **kw)


def _mod_kernel(s_ref, w_ref, b_ref, o_ref):
    a = _silu(s_ref[...]).astype(BF16)
    o_ref[...] = _dot(a, w_ref[...].astype(BF16)) + b_ref[...]


def _modulation(cond, w_ada, b_ada):
    tn = 1024
    n = 3 * D_MODEL
    return pl.pallas_call(
        _mod_kernel,
        grid=(DEPTH, n // tn),
        in_specs=[
            pl.BlockSpec((N_COND, D_MODEL), lambda l, j: (0, 0)),
            pl.BlockSpec((None, D_MODEL, tn), lambda l, j: (l, 0, j)),
            pl.BlockSpec((None, 1, tn), lambda l, j: (l, 0, j)),
        ],
        out_specs=pl.BlockSpec((None, N_COND, tn), lambda l, j: (l, 0, j)),
        out_shape=jax.ShapeDtypeStruct((DEPTH, N_COND, n), F32),
        compiler_params=_cparams(2),
        name="adaln_modulation",
    )(cond, w_ada, b_ada.reshape(DEPTH, 1, n))


def _norm_mod_kernel(x_ref, g_ref, shift_ref, scale_ref, o_ref, *, tm):
    grp = _cond_group(pl.program_id(0) * tm)
    x = x_ref[...]
    r = lax.rsqrt(jnp.mean(x * x, axis=-1, keepdims=True) + EPS)
    xn = x * r * g_ref[...]
    sh = shift_ref[pl.ds(grp, 1), :]
    sc = scale_ref[pl.ds(grp, 1), :]
    o_ref[...] = (xn * (1.0 + sc) + sh).astype(BF16)


def _norm_mod(x, g, shift, scale):
    tm = 512
    return pl.pallas_call(
        functools.partial(_norm_mod_kernel, tm=tm),
        grid=(T_ALL // tm,),
        in_specs=[
            pl.BlockSpec((tm, D_MODEL), lambda i: (i, 0)),
            pl.BlockSpec((1, D_MODEL), lambda i: (0, 0)),
            pl.BlockSpec((N_COND, D_MODEL), lambda i: (0, 0)),
            pl.BlockSpec((N_COND, D_MODEL), lambda i: (0, 0)),
        ],
        out_specs=pl.BlockSpec((tm, D_MODEL), lambda i: (i, 0)),
        out_shape=jax.ShapeDtypeStruct((T_ALL, D_MODEL), BF16),
        compiler_params=_cparams(1),
        name="norm_modulate",
    )(x, g.reshape(1, D_MODEL), shift, scale)


def _final_norm_kernel(x_ref, g_ref, ctx_ref, lat_ref, *, ctx_tiles):
    x = x_ref[...]
    r = lax.rsqrt(jnp.mean(x * x, axis=-1, keepdims=True) + EPS)
    y = x * r * g_ref[...]
    i = pl.program_id(0)

    @pl.when(i < ctx_tiles)
    def _():
        ctx_ref[...] = y

    @pl.when(i >= ctx_tiles)
    def _():
        lat_ref[...] = y


def _final_norm(x, g):
    tm = 512
    ctx_tiles = T_CTX // tm
    return pl.pallas_call(
        functools.partial(_final_norm_kernel, ctx_tiles=ctx_tiles),
        grid=(T_ALL // tm,),
        in_specs=[
            pl.BlockSpec((tm, D_MODEL), lambda i: (i, 0)),
            pl.BlockSpec((1, D_MODEL), lambda i: (0, 0)),
        ],
        out_specs=[
            pl.BlockSpec((tm, D_MODEL), lambda i: (jnp.minimum(i, ctx_tiles - 1), 0)),
            pl.BlockSpec((tm, D_MODEL), lambda i: (jnp.maximum(i - ctx_tiles, 0), 0)),
        ],
        out_shape=[
            jax.ShapeDtypeStruct((T_CTX, D_MODEL), F32),
            jax.ShapeDtypeStruct((T_LAT, D_MODEL), F32),
        ],
        compiler_params=_cparams(1),
        name="final_norm",
    )(x, g.reshape(1, D_MODEL))


def _proj_kernel(a_ref, w_ref, o_ref, wb_ref, *, w_is_nk):
    @pl.when(pl.program_id(1) == 0)
    def _():
        wb_ref[...] = w_ref[...].astype(BF16)

    if w_is_nk:
        o_ref[...] = _dot_nt(a_ref[...], wb_ref[...])
    else:
        o_ref[...] = _dot(a_ref[...], wb_ref[...])


def _project(a, w, layer, col0, n_cols, tn, w_is_nk=False, tm=512):
    k = a.shape[1]
    j0 = col0 // tn
    if w_is_nk:
        w_spec = pl.BlockSpec((None, tn, k), lambda j, i: (layer, j0 + j, 0))
        w_tile = (tn, k)
    else:
        w_spec = pl.BlockSpec((None, k, tn), lambda j, i: (layer, 0, j0 + j))
        w_tile = (k, tn)
    return pl.pallas_call(
        functools.partial(_proj_kernel, w_is_nk=w_is_nk),
        grid=(n_cols // tn, T_ALL // tm),
        in_specs=[pl.BlockSpec((tm, k), lambda j, i: (i, 0)), w_spec],
        out_specs=pl.BlockSpec((tm, tn), lambda j, i: (i, j)),
        out_shape=jax.ShapeDtypeStruct((T_ALL, n_cols), F32),
        scratch_shapes=[pltpu.VMEM(w_tile, BF16)],
        compiler_params=_cparams(2),
        name="in_projection",
    )(a, w)


def _proj_res_kernel(a_ref, w_ref, x_ref, gate_ref, o_ref, wb_ref, *, tm):
    i = pl.program_id(1)

    @pl.when(i == 0)
    def _():
        wb_ref[...] = w_ref[...].astype(BF16)

    gate = gate_ref[pl.ds(_cond_group(i * tm), 1), :]
    o_ref[...] = x_ref[...] + gate * _dot(a_ref[...], wb_ref[...])


def _project_residual(a, w, layer, x, gate):
    k = a.shape[1]
    tm, tn = 512, 512
    return pl.pallas_call(
        functools.partial(_proj_res_kernel, tm=tm),
        grid=(D_MODEL // tn, T_ALL // tm),
        in_specs=[
            pl.BlockSpec((tm, k), lambda j, i: (i, 0)),
            pl.BlockSpec((None, k, tn), lambda j, i: (layer, 0, j)),
            pl.BlockSpec((tm, tn), lambda j, i: (i, j)),
            pl.BlockSpec((N_COND, tn), lambda j, i: (0, j)),
        ],
        out_specs=pl.BlockSpec((tm, tn), lambda j, i: (i, j)),
        out_shape=jax.ShapeDtypeStruct((T_ALL, D_MODEL), F32),
        scratch_shapes=[pltpu.VMEM((k, tn), BF16)],
        compiler_params=_cparams(2),
        name="out_projection_residual",
    )(a, w, x, gate)


class _Slots:
    def __init__(self, chunk):
        self.chunk = chunk
        self.n = T_ALL // chunk
        self.ctx = T_CTX // chunk
        self.ctx_chunks = SEQ // chunk
        self.lat_chunks = DEC_SEQ // chunk

    def slot(self, d, g):
        return g + d * (self.n - 1 - 2 * g)

    def seq(self, j):
        return jnp.where(j < self.ctx, j // self.ctx_chunks, BATCH + (j - self.ctx) // self.lat_chunks)

    def flags(self, d, j):
        is_ctx = j < self.ctx
        pos = jnp.where(is_ctx, j % self.ctx_chunks, (j - self.ctx) % self.lat_chunks)
        n_chunks = jnp.where(is_ctx, self.ctx_chunks, self.lat_chunks)
        head_pos = jnp.where(d == 0, 0, n_chunks - 1)
        tail_pos = jnp.where(d == 0, n_chunks - 1, 0)
        return is_ctx, pos == head_pos, pos == tail_pos

    def mask(self, d):
        t_i = lax.broadcasted_iota(jnp.int32, (self.chunk, self.chunk), 0)
        s_i = lax.broadcasted_iota(jnp.int32, (self.chunk, self.chunk), 1)
        diff = (t_i - s_i) * (1 - 2 * d)
        return diff >= 0, diff > 0

    def lat_batch(self, d, g):
        return jnp.clip(self.seq(self.slot(d, g)) - BATCH, 0, DEC_BATCH - 1)

    def ctx_seq(self, d, g):
        return jnp.minimum(self.seq(self.slot(d, g)), BATCH - 1)


ML_SLOTS = _Slots(ML_CHUNK)
GD_SLOTS = _Slots(GD_CHUNK)


def _mlstm_kernel(q_ref, k_ref, v_ref, gc_ref, gr_ref, bc_ref, br_ref, c0_ref, n0_ref, m0_ref, *rest,
                  has_acc):
    if has_acc:
        rest = rest[1:]
    h_ref, cout_ref, nout_ref, mout_ref, c_s, n_s, m_s = rest
    L = ML_CHUNK
    d = pl.program_id(0)
    j = ML_SLOTS.slot(d, pl.program_id(1))
    is_ctx, is_head, is_tail = ML_SLOTS.flags(d, j)

    @pl.when(jnp.logical_and(is_head, is_ctx))
    def _():
        c_s[...] = jnp.zeros_like(c_s)
        n_s[...] = jnp.zeros_like(n_s)
        m_s[...] = jnp.zeros_like(m_s)

    @pl.when(jnp.logical_and(is_head, jnp.logical_not(is_ctx)))
    def _():
        c_s[...] = c0_ref[...]
        n_s[...] = n0_ref[...]
        m_s[...] = m0_ref[...]

    mask, _ = ML_SLOTS.mask(d)
    tri = mask.astype(F32)
    gc = gc_ref[...] + bc_ref[...]
    gr = gr_ref[...] + br_ref[...]
    i_c = gc[:, 0:ML_HEADS]
    b_c = _dot(tri, _log_sigmoid(gc[:, ML_HEADS:2 * ML_HEADS]), precision=HIGHEST)
    i_r = gr[0:ML_HEADS, :]
    b_r = _dot_nt(_log_sigmoid(gr[ML_HEADS:2 * ML_HEADS, :]), tri, precision=HIGHEST)
    bl_all = jnp.where(d == 0, b_c[L - 1:L, :], b_c[0:1, :])

    for h in range(ML_HEADS):
        q32 = q_ref[:, h * ML_DK:(h + 1) * ML_DK] * (ML_DK ** -0.5)
        k32 = k_ref[:, h * ML_DK:(h + 1) * ML_DK]
        qb = q32.astype(BF16)
        kb = k32.astype(BF16)
        vb = v_ref[:, h * ML_DV:(h + 1) * ML_DV].astype(BF16)
        bcol = b_c[:, h:h + 1]
        icol = i_c[:, h:h + 1]
        brow = b_r[h:h + 1, :]
        irow = i_r[h:h + 1, :]
        m = m_s[h:h + 1, 0:1]
        n = n_s[h:h + 1, :]

        dmat = jnp.where(mask, bcol - brow + irow, -jnp.inf)
        inter = bcol + m
        mt = jnp.maximum(inter, jnp.max(dmat, axis=-1, keepdims=True))
        s = _dot_nt(qb, kb) * jnp.exp(dmat - mt)
        sc = jnp.exp(inter - mt)
        num = sc * _dot(qb, c_s[h].astype(BF16)) + _dot(s.astype(BF16), vb)
        den = sc * jnp.sum(q32 * n, axis=-1, keepdims=True) + jnp.sum(s, axis=-1, keepdims=True)
        h_ref[:, h * ML_DV:(h + 1) * ML_DV] = num / jnp.maximum(jnp.abs(den), jnp.exp(-mt))

        bl = bl_all[:, h:h + 1]
        a = bl - bcol + icol
        m_new = jnp.maximum(bl + m, jnp.max(a, axis=0, keepdims=True))
        wk = jnp.exp(a - m_new) * k32
        wkb = wk.astype(BF16)
        dec = jnp.exp(bl + m - m_new)
        for cb in range(ML_DV // ML_CBLK):
            sl = slice(cb * ML_CBLK, (cb + 1) * ML_CBLK)
            c_s[h, :, sl] = dec * c_s[h, :, sl] + _dot_tn(wkb, vb[:, sl])
        n_s[h:h + 1, :] = dec * n + jnp.sum(wk, axis=0, keepdims=True)
        m_s[h:h + 1, :] = jnp.broadcast_to(m_new, (1, LANE))

    @pl.when(jnp.logical_and(is_tail, is_ctx))
    def _():
        cout_ref[...] = c_s[...]
        nout_ref[...] = n_s[...]
        mout_ref[...] = m_s[...]


def _mlstm_scan(proj, gates, b_gate, cache_c, cache_n, cache_m, j_layer, c_acc):
    H, L, S = ML_HEADS, ML_CHUNK, ML_SLOTS
    gd = jnp.stack([jnp.concatenate([gates[:, 0:H], gates[:, 2 * H:3 * H]], axis=1),
                    jnp.concatenate([gates[:, H:2 * H], gates[:, 3 * H:4 * H]], axis=1)], axis=0)
    gcol = gd.reshape(2, S.n, L, 2 * H)
    grow = gcol.transpose(0, 1, 3, 2)
    bd = jnp.stack([jnp.concatenate([b_gate[0:H], b_gate[2 * H:3 * H]]),
                    jnp.concatenate([b_gate[H:2 * H], b_gate[3 * H:4 * H]])], axis=0)
    bcol = bd.reshape(2, 1, 2 * H)
    brow = bd.reshape(2, 2 * H, 1)
    m0 = jnp.broadcast_to(cache_m[:, j_layer][..., None], (DEC_BATCH, 2, H, LANE))

    has_acc = c_acc is not None
    in_specs = [
        pl.BlockSpec((L, ML_QK), lambda d, g: (S.slot(d, g), 0)),
        pl.BlockSpec((L, ML_QK), lambda d, g: (S.slot(d, g), 1)),
        pl.BlockSpec((L, ML_INNER), lambda d, g: (S.slot(d, g), 1)),
        pl.BlockSpec((None, None, L, 2 * H), lambda d, g: (d, S.slot(d, g), 0, 0)),
        pl.BlockSpec((None, None, 2 * H, L), lambda d, g: (d, S.slot(d, g), 0, 0)),
        pl.BlockSpec((None, 1, 2 * H), lambda d, g: (d, 0, 0)),
        pl.BlockSpec((None, 2 * H, 1), lambda d, g: (d, 0, 0)),
        pl.BlockSpec((None, None, None, H, ML_DK, ML_DV), lambda d, g: (S.lat_batch(d, g), j_layer, d, 0, 0, 0)),
        pl.BlockSpec((None, None, None, H, ML_DK), lambda d, g: (S.lat_batch(d, g), j_layer, d, 0, 0)),
        pl.BlockSpec((None, None, H, LANE), lambda d, g: (S.lat_batch(d, g), d, 0, 0)),
    ]
    args = [proj, proj, proj, gcol, grow, bcol, brow, cache_c, cache_n, m0]
    aliases = {}
    if has_acc:
        in_specs.append(pl.BlockSpec(memory_space=pl.ANY))
        args.append(c_acc)
        aliases = {len(args) - 1: 1}

    return pl.pallas_call(
        functools.partial(_mlstm_kernel, has_acc=has_acc),
        grid=(2, S.n),
        in_specs=in_specs,
        out_specs=[
            pl.BlockSpec((None, L, ML_INNER), lambda d, g: (d, S.slot(d, g), 0)),
            pl.BlockSpec((None, None, None, H, ML_DK, ML_DV), lambda d, g: (S.ctx_seq(d, g), j_layer, d, 0, 0, 0)),
            pl.BlockSpec((None, None, H, ML_DK), lambda d, g: (S.ctx_seq(d, g), d, 0, 0)),
            pl.BlockSpec((None, None, H, LANE), lambda d, g: (S.ctx_seq(d, g), d, 0, 0)),
        ],
        out_shape=[
            jax.ShapeDtypeStruct((2, T_ALL, ML_INNER), F32),
            jax.ShapeDtypeStruct((BATCH, N_ML, 2, H, ML_DK, ML_DV), F32),
            jax.ShapeDtypeStruct((BATCH, 2, H, ML_DK), F32),
            jax.ShapeDtypeStruct((BATCH, 2, H, LANE), F32),
        ],
        scratch_shapes=[
            pltpu.VMEM((H, ML_DK, ML_DV), F32),
            pltpu.VMEM((H, ML_DK), F32),
            pltpu.VMEM((H, LANE), F32),
        ],
        input_output_aliases=aliases,
        compiler_params=_cparams(2),
        name="mlstm_scan",
    )(*args)


def _mlstm_post_kernel(hf_ref, hb_ref, o_ref, z_ref, g_ref, y_ref):
    hs = hf_ref[...] + hb_ref[...]
    r = lax.rsqrt(jnp.mean(hs * hs, axis=-1, keepdims=True) + EPS)
    y = (hs * r * g_ref[...]) * _sigmoid(o_ref[...]) * _silu(z_ref[...])
    y_ref[...] = y.astype(BF16)


def _mlstm_post(hdir, proj, g_head):
    tm = 512
    o_blk = (2 * ML_QK + ML_INNER) // ML_DV
    z_blk = (2 * ML_QK + 2 * ML_INNER) // ML_DV
    return pl.pallas_call(
        _mlstm_post_kernel,
        grid=(T_ALL // tm, ML_HEADS),
        in_specs=[
            pl.BlockSpec((None, tm, ML_DV), lambda i, h: (0, i, h)),
            pl.BlockSpec((None, tm, ML_DV), lambda i, h: (1, i, h)),
            pl.BlockSpec((tm, ML_DV), lambda i, h: (i, o_blk + h)),
            pl.BlockSpec((tm, ML_DV), lambda i, h: (i, z_blk + h)),
            pl.BlockSpec((1, ML_DV), lambda i, h: (0, h)),
        ],
        out_specs=pl.BlockSpec((tm, ML_DV), lambda i, h: (i, h)),
        out_shape=jax.ShapeDtypeStruct((T_ALL, ML_INNER), BF16),
        compiler_params=_cparams(2),
        name="mlstm_gate_norm",
    )(hdir, hdir, proj, proj, g_head.reshape(1, ML_INNER))


def _dwconv3_tile(x, w_ref, row_len, tm):
    r = lax.broadcasted_iota(jnp.int32, (tm, 1), 0) & (row_len - 1)
    prev = jnp.where(r == 0, 0.0, pltpu.roll(x, 1, axis=0))
    nxt = jnp.where(r == row_len - 1, 0.0, pltpu.roll(x, tm - 1, axis=0))
    return prev * w_ref[0:1, :] + x * w_ref[1:2, :] + nxt * w_ref[2:3, :]


def _sc_post_kernel(u_ref, b_ref, c_ref, z_ref, w_ref, y_ref, *, tm):
    row_len = _conv_row_len(pl.program_id(0) * tm)
    conv = _dwconv3_tile(c_ref[...] * u_ref[...], w_ref, row_len, tm)
    y_ref[...] = (b_ref[...] * conv * _silu(z_ref[...])).astype(BF16)


def _sc_post(proj, w_conv):
    tm, tn = 512, 512
    nb = SC_INNER // tn
    return pl.pallas_call(
        functools.partial(_sc_post_kernel, tm=tm),
        grid=(T_ALL // tm, nb),
        in_specs=[
            pl.BlockSpec((tm, tn), lambda i, j: (i, j)),
            pl.BlockSpec((tm, tn), lambda i, j: (i, nb + j)),
            pl.BlockSpec((tm, tn), lambda i, j: (i, 2 * nb + j)),
            pl.BlockSpec((tm, tn), lambda i, j: (i, 3 * nb + j)),
            pl.BlockSpec((3, tn), lambda i, j: (0, j)),
        ],
        out_specs=pl.BlockSpec((tm, tn), lambda i, j: (i, j)),
        out_shape=jax.ShapeDtypeStruct((T_ALL, SC_INNER), BF16),
        compiler_params=_cparams(2),
        name="shortconv_gate",
    )(proj, proj, proj, proj, w_conv)


def _gdn_prep_kernel(x_ref, w_ref, o_ref, *, tm, tn):
    j = pl.program_id(1)
    row_len = _conv_row_len(pl.program_id(0) * tm)
    y = _silu(_dwconv3_tile(x_ref[...], w_ref, row_len, tm))
    q_blocks = GD_QK // tn
    is_qk = j < 2 * q_blocks
    q_scale = jnp.where(j < q_blocks, GD_DK ** -0.5, 1.0)
    for hh in range(tn // GD_DK):
        yh = y[:, hh * GD_DK:(hh + 1) * GD_DK]
        r = lax.rsqrt(jnp.sum(yh * yh, axis=-1, keepdims=True) + EPS) * q_scale
        o_ref[:, hh * GD_DK:(hh + 1) * GD_DK] = yh * jnp.where(is_qk, r, 1.0)


def _gdn_prep(proj, w_conv):
    tm, tn = 512, 512
    return pl.pallas_call(
        functools.partial(_gdn_prep_kernel, tm=tm, tn=tn),
        grid=(T_ALL // tm, GD_CONV_CH // tn),
        in_specs=[
            pl.BlockSpec((tm, tn), lambda i, j: (i, j)),
            pl.BlockSpec((3, tn), lambda i, j: (0, j)),
        ],
        out_specs=pl.BlockSpec((tm, tn), lambda i, j: (i, j)),
        out_shape=jax.ShapeDtypeStruct((T_ALL, GD_CONV_CH), F32),
        compiler_params=_cparams(2),
        name="gdn_conv_norm",
    )(proj, w_conv)


def _gdn_kernel(q_ref, k_ref, v_ref, gc_ref, gr_ref, pc_ref, pr_ref, s0_ref,
                o_ref, sout_ref, s_s):
    L = GD_CHUNK
    d = pl.program_id(0)
    j = GD_SLOTS.slot(d, pl.program_id(2))
    is_ctx, is_head, is_tail = GD_SLOTS.flags(d, j)

    @pl.when(jnp.logical_and(is_head, is_ctx))
    def _():
        s_s[...] = jnp.zeros_like(s_s)

    @pl.when(jnp.logical_and(is_head, jnp.logical_not(is_ctx)))
    def _():
        s_s[...] = s0_ref[...]

    incl, strict = GD_SLOTS.mask(d)
    tri = incl.astype(F32)
    nh = GD_GV
    gc = gc_ref[...]
    gr = gr_ref[...]
    g_c = -jnp.exp(pc_ref[0:1, :]) * _softplus(gc[:, 0:nh] + pc_ref[1:2, :])
    G_c = _dot(tri, g_c, precision=HIGHEST)
    beta_c = _sigmoid(gc[:, nh:2 * nh])
    g_r = -jnp.exp(pr_ref[:, 0:1]) * _softplus(gr[0:nh, :] + pr_ref[:, 1:2])
    G_r = _dot_nt(g_r, tri, precision=HIGHEST)
    Gl_all = jnp.where(d == 0, G_c[L - 1:L, :], G_c[0:1, :])

    rep = GD_V_HEADS // GD_QK_HEADS
    heads = range(nh)
    k32, qb, kk, qk = [], [], [], []
    for hq in range(GD_GQ):
        k32.append(k_ref[:, hq * GD_DK:(hq + 1) * GD_DK])
        qb.append(q_ref[:, hq * GD_DK:(hq + 1) * GD_DK].astype(BF16))
        kb = k32[hq].astype(BF16)
        kk.append(_dot_nt(kb, kb))
        qk.append(_dot_nt(qb[hq], kb))

    decay, neg_a, rhs, eG = [], [], [], []
    for hv in heads:
        Gc = G_c[:, hv:hv + 1]
        beta = beta_c[:, hv:hv + 1]
        decay.append(jnp.where(incl, jnp.exp(jnp.where(incl, Gc - G_r[hv:hv + 1, :], 0.0)), 0.0))
        neg_a.append(jnp.where(strict, -(beta * kk[hv // rep] * decay[hv]), 0.0))
        eG.append(jnp.exp(Gc))
        v32 = v_ref[:, hv * GD_DV:(hv + 1) * GD_DV]
        rhs.append(jnp.concatenate([(beta * eG[hv]) * k32[hv // rep], beta * v32], axis=-1))

    pw = [a.astype(BF16) for a in neg_a]
    m_inv = list(neg_a)
    for _ in range(5):
        pw32 = [_dot(p, p) for p in pw]
        pw = [p.astype(BF16) for p in pw32]
        m_inv = [m + p32 + _dot(m.astype(BF16), p) for m, p32, p in zip(m_inv, pw32, pw)]
    mb = [m.astype(BF16) for m in m_inv]
    x1 = [r + _dot(m, r.astype(BF16)) for m, r in zip(mb, rhs)]
    res = [(r - x) + _dot(a, x, precision=HIGHEST) for r, x, a in zip(rhs, x1, neg_a)]
    sol = [x + r + _dot(m, r.astype(BF16)) for x, r, m in zip(x1, res, mb)]

    for hv in heads:
        S = s_s[hv]
        Sb = S.astype(BF16)
        Gc = G_c[:, hv:hv + 1]
        Gl = Gl_all[:, hv:hv + 1]
        u = sol[hv][:, GD_DK:GD_DK + GD_DV] - _dot(sol[hv][:, 0:GD_DK].astype(BF16), Sb)
        ub = u.astype(BF16)
        p_mat = (qk[hv // rep] * decay[hv]).astype(BF16)
        o_ref[:, hv * GD_DV:(hv + 1) * GD_DV] = eG[hv] * _dot(qb[hv // rep], Sb) + _dot(p_mat, ub)
        kdec = (jnp.exp(Gl - Gc) * k32[hv // rep]).astype(BF16)
        s_s[hv] = jnp.exp(Gl) * S + _dot_tn(kdec, ub)

    @pl.when(jnp.logical_and(is_tail, is_ctx))
    def _():
        sout_ref[...] = s_s[...]


def _gdn_scan(qkv, ab, a_log, dt_bias, cache_s, j_layer):
    HV, nh, L, S = GD_V_HEADS, GD_GV, GD_CHUNK, GD_SLOTS
    a_d = ab[:, 0:2 * HV].reshape(T_ALL, 2, GD_GROUPS, nh)
    b_d = ab[:, 2 * HV:4 * HV].reshape(T_ALL, 2, GD_GROUPS, nh)
    gcol = jnp.concatenate([a_d, b_d], axis=-1).reshape(S.n, L, 2, GD_GROUPS, 2 * nh)
    gcol = gcol.transpose(2, 3, 0, 1, 4)
    grow = gcol.transpose(0, 1, 2, 4, 3)
    par = jnp.stack([a_log.astype(F32), dt_bias.astype(F32)], axis=1)
    pcol = par.reshape(2, 2, GD_GROUPS, nh).transpose(0, 2, 1, 3)
    prow = pcol.transpose(0, 1, 3, 2)

    qb = GD_GQ * GD_DK
    vb = nh * GD_DV
    return pl.pallas_call(
        _gdn_kernel,
        grid=(2, GD_GROUPS, S.n),
        in_specs=[
            pl.BlockSpec((L, qb), lambda d, p, g: (S.slot(d, g), p)),
            pl.BlockSpec((L, qb), lambda d, p, g: (S.slot(d, g), GD_GROUPS + p)),
            pl.BlockSpec((L, vb), lambda d, p, g: (S.slot(d, g), GD_GROUPS + p)),
            pl.BlockSpec((None, None, None, L, 2 * nh), lambda d, p, g: (d, p, S.slot(d, g), 0, 0)),
            pl.BlockSpec((None, None, None, 2 * nh, L), lambda d, p, g: (d, p, S.slot(d, g), 0, 0)),
            pl.BlockSpec((None, None, 2, nh), lambda d, p, g: (d, p, 0, 0)),
            pl.BlockSpec((None, None, nh, 2), lambda d, p, g: (d, p, 0, 0)),
            pl.BlockSpec((None, None, None, nh, GD_DK, GD_DV),
                         lambda d, p, g: (S.lat_batch(d, g), j_layer, d, p, 0, 0)),
        ],
        out_specs=[
            pl.BlockSpec((None, L, vb), lambda d, p, g: (d, S.slot(d, g), p)),
            pl.BlockSpec((None, None, nh, GD_DK, GD_DV), lambda d, p, g: (S.ctx_seq(d, g), d, p, 0, 0)),
        ],
        out_shape=[
            jax.ShapeDtypeStruct((2, T_ALL, GD_INNER), F32),
            jax.ShapeDtypeStruct((BATCH, 2, HV, GD_DK, GD_DV), F32),
        ],
        scratch_shapes=[pltpu.VMEM((nh, GD_DK, GD_DV), F32)],
        compiler_params=_cparams(3),
        name="gdn_scan",
    )(qkv, qkv, qkv, gcol, grow, pcol, prow, cache_s)


def _gdn_post_kernel(of_ref, ob_ref, z_ref, g_ref, y_ref, *, tn):
    g = g_ref[...]
    for hh in range(tn // GD_DV):
        sl = slice(hh * GD_DV, (hh + 1) * GD_DV)
        o = of_ref[:, sl] + ob_ref[:, sl]
        r = lax.rsqrt(jnp.mean(o * o, axis=-1, keepdims=True) + EPS)
        y_ref[:, sl] = ((o * r * g) * _silu(z_ref[:, sl])).astype(BF16)


def _gdn_post(odir, proj, g_norm):
    tm, tn = 512, 512
    z_blk = GD_CONV_CH // tn
    return pl.pallas_call(
        functools.partial(_gdn_post_kernel, tn=tn),
        grid=(T_ALL // tm, GD_INNER // tn),
        in_specs=[
            pl.BlockSpec((None, tm, tn), lambda i, j: (0, i, j)),
            pl.BlockSpec((None, tm, tn), lambda i, j: (1, i, j)),
            pl.BlockSpec((tm, tn), lambda i, j: (i, z_blk + j)),
            pl.BlockSpec((1, GD_DV), lambda i, j: (0, 0)),
        ],
        out_specs=pl.BlockSpec((tm, tn), lambda i, j: (i, j)),
        out_shape=jax.ShapeDtypeStruct((T_ALL, GD_INNER), BF16),
        compiler_params=_cparams(2),
        name="gdn_gate_norm",
    )(odir, odir, proj, g_norm.reshape(1, GD_DV))


def kernel(x_prompt, x_sample, c, cache_ml_C, cache_ml_n, cache_ml_m, cache_gd_S, c_ctx, w_ada, b_ada, g_norm,
           w_ml_in, b_ml_gate, g_ml_head, w_ml_out, w_sc_in, w_sc_conv, w_sc_out, w_gd_in, w_gd_conv, gd_A_log,
           gd_dt_bias, g_gd_norm, w_gd_out, g_final):
    x = jnp.concatenate([x_prompt.reshape(T_CTX, D_MODEL), x_sample.reshape(T_LAT, D_MODEL)], axis=0)
    cond = jnp.concatenate([c_ctx[None, :], c, jnp.zeros((N_COND - 1 - DEC_BATCH, D_MODEL), F32)], axis=0)
    mod = _modulation(cond, w_ada, b_ada)
    w_ml_nk = jnp.swapaxes(w_ml_in, 1, 2)

    state_ml_c = None
    ml_n, ml_m, gd_s = [], [], []
    for l in range(DEPTH):
        shift = mod[l, :, 0:D_MODEL]
        scale = mod[l, :, D_MODEL:2 * D_MODEL]
        gate = mod[l, :, 2 * D_MODEL:3 * D_MODEL]
        h = _norm_mod(x, g_norm[l], shift, scale)
        j = l // 3
        kind = l % 3
        if kind == 0:
            proj = _project(h, w_ml_nk, j, 0, ML_MAIN, 1024, w_is_nk=True)
            gates = _project(h, w_ml_nk, j, ML_MAIN, 4 * ML_HEADS, 4 * ML_HEADS, w_is_nk=True)
            hdir, state_ml_c, n_fin, m_fin = _mlstm_scan(proj, gates, b_ml_gate[j], cache_ml_C, cache_ml_n,
                                                         cache_ml_m, j, state_ml_c)
            ml_n.append(n_fin)
            ml_m.append(m_fin[..., 0])
            y = _mlstm_post(hdir, proj, g_ml_head[j])
            x = _project_residual(y, w_ml_out, j, x, gate)
        elif kind == 1:
            proj = _project(h, w_sc_in, j, 0, 4 * SC_INNER, 1024)
            y = _sc_post(proj, w_sc_conv[j])
            x = _project_residual(y, w_sc_out, j, x, gate)
        else:
            proj = _project(h, w_gd_in, j, 0, GD_MAIN, 1024)
            ab = _project(h, w_gd_in, j, GD_MAIN, 4 * GD_V_HEADS, 4 * GD_V_HEADS)
            qkv = _gdn_prep(proj, w_gd_conv[j])
            odir, s_fin = _gdn_scan(qkv, ab, gd_A_log[j], gd_dt_bias[j], cache_gd_S, j)
            gd_s.append(s_fin)
            y = _gdn_post(odir, proj, g_gd_norm[j])
            x = _project_residual(y, w_gd_out, j, x, gate)

    y_ctx, y_lat = _final_norm(x, g_final)
    y_prompt = y_ctx.reshape(BATCH, SEQ, D_MODEL)
    y_sample = y_lat.reshape(DEC_BATCH, DEC_SEQ, D_MODEL)
    state_ml_n = jnp.stack(ml_n, axis=1)
    state_ml_m = jnp.stack(ml_m, axis=1)
    state_gd_s = jnp.stack(gd_s, axis=1)
    return (y_prompt, y_sample, state_ml_c, state_ml_n, state_ml_m, state_gd_s)
```

```python
import functools

import jax
import jax.numpy as jnp
from jax import lax
from jax.experimental import pallas as pl
from jax.experimental.pallas import tpu as pltpu

F32 = jnp.float32
BF16 = jnp.bfloat16
HIGHEST = lax.Precision.HIGHEST

D_MODEL = 2048
BATCH = 16
SEQ = 256
DEPTH = 4
DEC_BATCH = 2
DEC_SEQ = 2048
GRID_W = 64
EPS = 1e-6

T_CTX = BATCH * SEQ
T_LAT = DEC_BATCH * DEC_SEQ
T_ALL = T_CTX + T_LAT
N_COND = 8

ML_HEADS = 8
ML_DK = 256
ML_DV = 512
ML_QK = ML_HEADS * ML_DK
ML_INNER = ML_HEADS * ML_DV
ML_MAIN = 2 * ML_QK + 3 * ML_INNER
N_ML = 2
ML_CHUNK = 256
ML_CBLK = 256

SC_INNER = 2 * D_MODEL

GD_DK = 128
GD_DV = 128
GD_QK_HEADS = 16
GD_V_HEADS = 32
GD_QK = GD_QK_HEADS * GD_DK
GD_INNER = GD_V_HEADS * GD_DV
GD_CONV_CH = 2 * GD_QK + GD_INNER
GD_MAIN = GD_CONV_CH + GD_INNER
GD_GROUPS = 1
GD_GV = GD_V_HEADS // GD_GROUPS
GD_GQ = GD_QK_HEADS // GD_GROUPS
GD_CHUNK = 64

LANE = 128
VMEM_LIMIT = 56 * 1024 * 1024


def _cparams(n_axes):
    return pltpu.CompilerParams(dimension_semantics=("arbitrary",) * n_axes, vmem_limit_bytes=VMEM_LIMIT)


def _sigmoid(x):
    return 1.0 / (1.0 + jnp.exp(-x))


def _silu(x):
    return x * _sigmoid(x)


def _softplus(x):
    return jnp.maximum(x, 0.0) + jnp.log1p(jnp.exp(-jnp.abs(x)))


def _log_sigmoid(x):
    return -_softplus(-x)


def _cond_group(row0):
    return jnp.where(row0 < T_CTX, 0, 1 + (row0 - T_CTX) // DEC_SEQ)


def _conv_row_len(row0):
    return jnp.where(row0 < T_CTX, SEQ, GRID_W)


def _dot_nt(a, b, **kw):
    return lax.dot_general(a, b, (((1,), (1,)), ((), ())), preferred_element_type=F32, **kw)


def _dot_tn(a, b, **kw):
    return lax.dot_general(a, b, (((0,), (0,)), ((), ())), preferred_element_type=F32, **kw)


def _dot(a, b, **kw):
    return jnp.dot(a, b, preferred_element_type=F32, **kw)


def _mod_kernel(s_ref, w_ref, b_ref, o_ref):
    a = _silu(s_ref[...]).astype(BF16)
    o_ref[...] = _dot(a, w_ref[...].astype(BF16)) + b_ref[...]


def _modulation(cond, w_ada, b_ada):
    tn = 1024
    n = 3 * D_MODEL
    return pl.pallas_call(
        _mod_kernel,
        grid=(DEPTH, n // tn),
        in_specs=[
            pl.BlockSpec((N_COND, D_MODEL), lambda l, j: (0, 0)),
            pl.BlockSpec((None, D_MODEL, tn), lambda l, j: (l, 0, j)),
            pl.BlockSpec((None, 1, tn), lambda l, j: (l, 0, j)),
        ],
        out_specs=pl.BlockSpec((None, N_COND, tn), lambda l, j: (l, 0, j)),
        out_shape=jax.ShapeDtypeStruct((DEPTH, N_COND, n), F32),
        compiler_params=_cparams(2),
        name="adaln_modulation",
    )(cond, w_ada, b_ada.reshape(DEPTH, 1, n))


def _norm_mod_kernel(x_ref, g_ref, shift_ref, scale_ref, o_ref, *, tm):
    grp = _cond_group(pl.program_id(0) * tm)
    x = x_ref[...]
    r = lax.rsqrt(jnp.mean(x * x, axis=-1, keepdims=True) + EPS)
    xn = x * r * g_ref[...]
    sh = shift_ref[pl.ds(grp, 1), :]
    sc = scale_ref[pl.ds(grp, 1), :]
    o_ref[...] = (xn * (1.0 + sc) + sh).astype(BF16)


def _norm_mod(x, g, shift, scale):
    tm = 512
    return pl.pallas_call(
        functools.partial(_norm_mod_kernel, tm=tm),
        grid=(T_ALL // tm,),
        in_specs=[
            pl.BlockSpec((tm, D_MODEL), lambda i: (i, 0)),
            pl.BlockSpec((1, D_MODEL), lambda i: (0, 0)),
            pl.BlockSpec((N_COND, D_MODEL), lambda i: (0, 0)),
            pl.BlockSpec((N_COND, D_MODEL), lambda i: (0, 0)),
        ],
        out_specs=pl.BlockSpec((tm, D_MODEL), lambda i: (i, 0)),
        out_shape=jax.ShapeDtypeStruct((T_ALL, D_MODEL), BF16),
        compiler_params=_cparams(1),
        name="norm_modulate",
    )(x, g.reshape(1, D_MODEL), shift, scale)


def _final_norm_kernel(x_ref, g_ref, ctx_ref, lat_ref, *, ctx_tiles):
    x = x_ref[...]
    r = lax.rsqrt(jnp.mean(x * x, axis=-1, keepdims=True) + EPS)
    y = x * r * g_ref[...]
    i = pl.program_id(0)

    @pl.when(i < ctx_tiles)
    def _():
        ctx_ref[...] = y

    @pl.when(i >= ctx_tiles)
    def _():
        lat_ref[...] = y


def _final_norm(x, g):
    tm = 512
    ctx_tiles = T_CTX // tm
    return pl.pallas_call(
        functools.partial(_final_norm_kernel, ctx_tiles=ctx_tiles),
        grid=(T_ALL // tm,),
        in_specs=[
            pl.BlockSpec((tm, D_MODEL), lambda i: (i, 0)),
            pl.BlockSpec((1, D_MODEL), lambda i: (0, 0)),
        ],
        out_specs=[
            pl.BlockSpec((tm, D_MODEL), lambda i: (jnp.minimum(i, ctx_tiles - 1), 0)),
            pl.BlockSpec((tm, D_MODEL), lambda i: (jnp.maximum(i - ctx_tiles, 0), 0)),
        ],
        out_shape=[
            jax.ShapeDtypeStruct((T_CTX, D_MODEL), F32),
            jax.ShapeDtypeStruct((T_LAT, D_MODEL), F32),
        ],
        compiler_params=_cparams(1),
        name="final_norm",
    )(x, g.reshape(1, D_MODEL))


def _proj_kernel(a_ref, w_ref, o_ref, wb_ref, *, w_is_nk):
    @pl.when(pl.program_id(1) == 0)
    def _():
        wb_ref[...] = w_ref[...].astype(BF16)

    dot = _dot_nt if w_is_nk else _dot
    o_ref[...] = dot(a_ref[...], wb_ref[...]).astype(o_ref.dtype)


def _project(a, w, layer, col0, n_cols, tn, out_dtype, w_is_nk=False, tm=1024):
    k = a.shape[1]
    j0 = col0 // tn
    if w_is_nk:
        w_spec = pl.BlockSpec((None, tn, k), lambda j, i: (layer, j0 + j, 0))
        w_tile = (tn, k)
    else:
        w_spec = pl.BlockSpec((None, k, tn), lambda j, i: (layer, 0, j0 + j))
        w_tile = (k, tn)
    return pl.pallas_call(
        functools.partial(_proj_kernel, w_is_nk=w_is_nk),
        grid=(n_cols // tn, T_ALL // tm),
        in_specs=[pl.BlockSpec((tm, k), lambda j, i: (i, 0)), w_spec],
        out_specs=pl.BlockSpec((tm, tn), lambda j, i: (i, j)),
        out_shape=jax.ShapeDtypeStruct((T_ALL, n_cols), out_dtype),
        scratch_shapes=[pltpu.VMEM(w_tile, BF16)],
        compiler_params=_cparams(2),
        name="in_projection",
    )(a, w)


def _proj_res_kernel(a_ref, w_ref, x_ref, gate_ref, o_ref, wb_ref, *, tm):
    i = pl.program_id(1)

    @pl.when(i == 0)
    def _():
        wb_ref[...] = w_ref[...].astype(BF16)

    gate = gate_ref[pl.ds(_cond_group(i * tm), 1), :]
    o_ref[...] = x_ref[...] + gate * _dot(a_ref[...], wb_ref[...])


def _project_residual(a, w, layer, x, gate):
    k = a.shape[1]
    tm, tn = 1024, 512
    return pl.pallas_call(
        functools.partial(_proj_res_kernel, tm=tm),
        grid=(D_MODEL // tn, T_ALL // tm),
        in_specs=[
            pl.BlockSpec((tm, k), lambda j, i: (i, 0)),
            pl.BlockSpec((None, k, tn), lambda j, i: (layer, 0, j)),
            pl.BlockSpec((tm, tn), lambda j, i: (i, j)),
            pl.BlockSpec((N_COND, tn), lambda j, i: (0, j)),
        ],
        out_specs=pl.BlockSpec((tm, tn), lambda j, i: (i, j)),
        out_shape=jax.ShapeDtypeStruct((T_ALL, D_MODEL), F32),
        scratch_shapes=[pltpu.VMEM((k, tn), BF16)],
        compiler_params=_cparams(2),
        name="out_projection_residual",
    )(a, w, x, gate)


class _Slots:
    def __init__(self, chunk):
        self.chunk = chunk
        self.n = T_ALL // chunk
        self.ctx = T_CTX // chunk
        self.ctx_chunks = SEQ // chunk
        self.lat_chunks = DEC_SEQ // chunk

    def slot(self, d, g):
        return g + d * (self.n - 1 - 2 * g)

    def seq(self, j):
        return jnp.where(j < self.ctx, j // self.ctx_chunks, BATCH + (j - self.ctx) // self.lat_chunks)

    def flags(self, d, j):
        is_ctx = j < self.ctx
        pos = jnp.where(is_ctx, j % self.ctx_chunks, (j - self.ctx) % self.lat_chunks)
        n_chunks = jnp.where(is_ctx, self.ctx_chunks, self.lat_chunks)
        head_pos = jnp.where(d == 0, 0, n_chunks - 1)
        tail_pos = jnp.where(d == 0, n_chunks - 1, 0)
        return is_ctx, pos == head_pos, pos == tail_pos

    def mask(self, d):
        t_i = lax.broadcasted_iota(jnp.int32, (self.chunk, self.chunk), 0)
        s_i = lax.broadcasted_iota(jnp.int32, (self.chunk, self.chunk), 1)
        diff = (t_i - s_i) * (1 - 2 * d)
        return diff >= 0, diff > 0

    def lat_batch(self, d, g):
        return jnp.clip(self.seq(self.slot(d, g)) - BATCH, 0, DEC_BATCH - 1)

    def ctx_seq(self, d, g):
        return jnp.minimum(self.seq(self.slot(d, g)), BATCH - 1)


ML_SLOTS = _Slots(ML_CHUNK)
GD_SLOTS = _Slots(GD_CHUNK)


def _mlstm_kernel(q_ref, k_ref, v_ref, gc_ref, gr_ref, bc_ref, br_ref, c0_ref, n0_ref, m0_ref, *rest,
                  has_acc):
    if has_acc:
        rest = rest[1:]
    h_ref, cout_ref, nout_ref, mout_ref, c_s, n_s, m_s = rest
    L = ML_CHUNK
    d = pl.program_id(0)
    j = ML_SLOTS.slot(d, pl.program_id(1))
    is_ctx, is_head, is_tail = ML_SLOTS.flags(d, j)

    @pl.when(jnp.logical_and(is_head, is_ctx))
    def _():
        c_s[...] = jnp.zeros_like(c_s)
        n_s[...] = jnp.zeros_like(n_s)
        m_s[...] = jnp.zeros_like(m_s)

    @pl.when(jnp.logical_and(is_head, jnp.logical_not(is_ctx)))
    def _():
        c_s[...] = c0_ref[...]
        n_s[...] = n0_ref[...]
        m_s[...] = m0_ref[...]

    mask, _ = ML_SLOTS.mask(d)
    tri = mask.astype(F32)
    gc = gc_ref[...] + bc_ref[...]
    gr = gr_ref[...] + br_ref[...]
    i_c = gc[:, 0:ML_HEADS]
    b_c = _dot(tri, _log_sigmoid(gc[:, ML_HEADS:2 * ML_HEADS]), precision=HIGHEST)
    i_r = gr[0:ML_HEADS, :]
    b_r = _dot_nt(_log_sigmoid(gr[ML_HEADS:2 * ML_HEADS, :]), tri, precision=HIGHEST)
    bl_all = jnp.where(d == 0, b_c[L - 1:L, :], b_c[0:1, :])

    a_all, m_new_all, dec_all = [], [], []
    for h in range(ML_HEADS):
        qb = q_ref[:, h * ML_DK:(h + 1) * ML_DK] * (ML_DK ** -0.5)
        q32 = qb.astype(F32)
        kb = k_ref[:, h * ML_DK:(h + 1) * ML_DK]
        vb = v_ref[:, h * ML_DV:(h + 1) * ML_DV]
        bcol = b_c[:, h:h + 1]
        icol = i_c[:, h:h + 1]
        brow = b_r[h:h + 1, :]
        irow = i_r[h:h + 1, :]
        m = m_s[h:h + 1, 0:1]
        n = n_s[h:h + 1, :]

        dmat = jnp.where(mask, bcol - brow + irow, -jnp.inf)
        inter = bcol + m
        mt = jnp.maximum(inter, jnp.max(dmat, axis=-1, keepdims=True))
        s = _dot_nt(qb, kb) * jnp.exp(dmat - mt)
        sc = jnp.exp(inter - mt)
        num = sc * _dot(qb, c_s[h].astype(BF16)) + _dot(s.astype(BF16), vb)
        den = sc * jnp.sum(q32 * n, axis=-1, keepdims=True) + jnp.sum(s, axis=-1, keepdims=True)
        h_ref[:, h * ML_DV:(h + 1) * ML_DV] = (num / jnp.maximum(jnp.abs(den), jnp.exp(-mt))).astype(h_ref.dtype)

        bl = bl_all[:, h:h + 1]
        a = bl - bcol + icol
        m_new = jnp.maximum(bl + m, jnp.max(a, axis=0, keepdims=True))
        a_all.append(a)
        m_new_all.append(m_new)
        dec_all.append(jnp.exp(bl + m - m_new))

    for h in range(ML_HEADS):
        wk = jnp.exp(a_all[h] - m_new_all[h]) * k_ref[:, h * ML_DK:(h + 1) * ML_DK].astype(F32)
        wkb = wk.astype(BF16)
        dec = dec_all[h]
        for cb in range(ML_DV // ML_CBLK):
            sl = slice(h * ML_DV + cb * ML_CBLK, h * ML_DV + (cb + 1) * ML_CBLK)
            csl = slice(cb * ML_CBLK, (cb + 1) * ML_CBLK)
            c_s[h, :, csl] = dec * c_s[h, :, csl] + _dot_tn(wkb, v_ref[:, sl])
        n_s[h:h + 1, :] = dec * n_s[h:h + 1, :] + jnp.sum(wk, axis=0, keepdims=True)
        m_s[h:h + 1, :] = jnp.broadcast_to(m_new_all[h], (1, LANE))

    @pl.when(jnp.logical_and(is_tail, is_ctx))
    def _():
        cout_ref[...] = c_s[...]
        nout_ref[...] = n_s[...]
        mout_ref[...] = m_s[...]


def _mlstm_scan(proj, gates, b_gate, cache_c, cache_n, cache_m, j_layer, c_acc):
    H, L, S = ML_HEADS, ML_CHUNK, ML_SLOTS
    gd = jnp.stack([jnp.concatenate([gates[:, 0:H], gates[:, 2 * H:3 * H]], axis=1),
                    jnp.concatenate([gates[:, H:2 * H], gates[:, 3 * H:4 * H]], axis=1)], axis=0)
    gcol = gd.reshape(2, S.n, L, 2 * H)
    grow = gcol.transpose(0, 1, 3, 2)
    bd = jnp.stack([jnp.concatenate([b_gate[0:H], b_gate[2 * H:3 * H]]),
                    jnp.concatenate([b_gate[H:2 * H], b_gate[3 * H:4 * H]])], axis=0)
    bcol = bd.reshape(2, 1, 2 * H)
    brow = bd.reshape(2, 2 * H, 1)
    m0 = jnp.broadcast_to(cache_m[:, j_layer][..., None], (DEC_BATCH, 2, H, LANE))

    has_acc = c_acc is not None
    in_specs = [
        pl.BlockSpec((L, ML_QK), lambda d, g: (S.slot(d, g), 0)),
        pl.BlockSpec((L, ML_QK), lambda d, g: (S.slot(d, g), 1)),
        pl.BlockSpec((L, ML_INNER), lambda d, g: (S.slot(d, g), 1)),
        pl.BlockSpec((None, None, L, 2 * H), lambda d, g: (d, S.slot(d, g), 0, 0)),
        pl.BlockSpec((None, None, 2 * H, L), lambda d, g: (d, S.slot(d, g), 0, 0)),
        pl.BlockSpec((None, 1, 2 * H), lambda d, g: (d, 0, 0)),
        pl.BlockSpec((None, 2 * H, 1), lambda d, g: (d, 0, 0)),
        pl.BlockSpec((None, None, None, H, ML_DK, ML_DV), lambda d, g: (S.lat_batch(d, g), j_layer, d, 0, 0, 0)),
        pl.BlockSpec((None, None, None, H, ML_DK), lambda d, g: (S.lat_batch(d, g), j_layer, d, 0, 0)),
        pl.BlockSpec((None, None, H, LANE), lambda d, g: (S.lat_batch(d, g), d, 0, 0)),
    ]
    args = [proj, proj, proj, gcol, grow, bcol, brow, cache_c, cache_n, m0]
    aliases = {}
    if has_acc:
        in_specs.append(pl.BlockSpec(memory_space=pl.ANY))
        args.append(c_acc)
        aliases = {len(args) - 1: 1}

    return pl.pallas_call(
        functools.partial(_mlstm_kernel, has_acc=has_acc),
        grid=(2, S.n),
        in_specs=in_specs,
        out_specs=[
            pl.BlockSpec((None, L, ML_INNER), lambda d, g: (d, S.slot(d, g), 0)),
            pl.BlockSpec((None, None, None, H, ML_DK, ML_DV), lambda d, g: (S.ctx_seq(d, g), j_layer, d, 0, 0, 0)),
            pl.BlockSpec((None, None, H, ML_DK), lambda d, g: (S.ctx_seq(d, g), d, 0, 0)),
            pl.BlockSpec((None, None, H, LANE), lambda d, g: (S.ctx_seq(d, g), d, 0, 0)),
        ],
        out_shape=[
            jax.ShapeDtypeStruct((2, T_ALL, ML_INNER), BF16),
            jax.ShapeDtypeStruct((BATCH, N_ML, 2, H, ML_DK, ML_DV), F32),
            jax.ShapeDtypeStruct((BATCH, 2, H, ML_DK), F32),
            jax.ShapeDtypeStruct((BATCH, 2, H, LANE), F32),
        ],
        scratch_shapes=[
            pltpu.VMEM((H, ML_DK, ML_DV), F32),
            pltpu.VMEM((H, ML_DK), F32),
            pltpu.VMEM((H, LANE), F32),
        ],
        input_output_aliases=aliases,
        compiler_params=_cparams(2),
        name="mlstm_scan",
    )(*args)


def _mlstm_post_kernel(hf_ref, hb_ref, o_ref, z_ref, g_ref, y_ref):
    hs = hf_ref[...].astype(F32) + hb_ref[...].astype(F32)
    r = lax.rsqrt(jnp.mean(hs * hs, axis=-1, keepdims=True) + EPS)
    y = (hs * r * g_ref[...]) * _sigmoid(o_ref[...].astype(F32)) * _silu(z_ref[...].astype(F32))
    y_ref[...] = y.astype(BF16)


def _mlstm_post(hdir, proj, g_head):
    tm = 512
    o_blk = (2 * ML_QK + ML_INNER) // ML_DV
    z_blk = (2 * ML_QK + 2 * ML_INNER) // ML_DV
    return pl.pallas_call(
        _mlstm_post_kernel,
        grid=(T_ALL // tm, ML_HEADS),
        in_specs=[
            pl.BlockSpec((None, tm, ML_DV), lambda i, h: (0, i, h)),
            pl.BlockSpec((None, tm, ML_DV), lambda i, h: (1, i, h)),
            pl.BlockSpec((tm, ML_DV), lambda i, h: (i, o_blk + h)),
            pl.BlockSpec((tm, ML_DV), lambda i, h: (i, z_blk + h)),
            pl.BlockSpec((1, ML_DV), lambda i, h: (0, h)),
        ],
        out_specs=pl.BlockSpec((tm, ML_DV), lambda i, h: (i, h)),
        out_shape=jax.ShapeDtypeStruct((T_ALL, ML_INNER), BF16),
        compiler_params=_cparams(2),
        name="mlstm_gate_norm",
    )(hdir, hdir, proj, proj, g_head.reshape(1, ML_INNER))


def _dwconv3_tile(x, w_ref, row_len, tm):
    r = lax.broadcasted_iota(jnp.int32, (tm, 1), 0) & (row_len - 1)
    prev = jnp.where(r == 0, 0.0, pltpu.roll(x, 1, axis=0))
    nxt = jnp.where(r == row_len - 1, 0.0, pltpu.roll(x, tm - 1, axis=0))
    return prev * w_ref[0:1, :] + x * w_ref[1:2, :] + nxt * w_ref[2:3, :]


def _sc_post_kernel(u_ref, b_ref, c_ref, z_ref, w_ref, y_ref, *, tm):
    row_len = _conv_row_len(pl.program_id(0) * tm)
    conv = _dwconv3_tile(c_ref[...].astype(F32) * u_ref[...].astype(F32), w_ref, row_len, tm)
    y_ref[...] = (b_ref[...].astype(F32) * conv * _silu(z_ref[...].astype(F32))).astype(BF16)


def _sc_post(proj, w_conv):
    tm, tn = 512, 512
    nb = SC_INNER // tn
    return pl.pallas_call(
        functools.partial(_sc_post_kernel, tm=tm),
        grid=(T_ALL // tm, nb),
        in_specs=[
            pl.BlockSpec((tm, tn), lambda i, j: (i, j)),
            pl.BlockSpec((tm, tn), lambda i, j: (i, nb + j)),
            pl.BlockSpec((tm, tn), lambda i, j: (i, 2 * nb + j)),
            pl.BlockSpec((tm, tn), lambda i, j: (i, 3 * nb + j)),
            pl.BlockSpec((3, tn), lambda i, j: (0, j)),
        ],
        out_specs=pl.BlockSpec((tm, tn), lambda i, j: (i, j)),
        out_shape=jax.ShapeDtypeStruct((T_ALL, SC_INNER), BF16),
        compiler_params=_cparams(2),
        name="shortconv_gate",
    )(proj, proj, proj, proj, w_conv)


def _gdn_prep_kernel(x_ref, w_ref, o_ref, *, tm, tn):
    j = pl.program_id(1)
    row_len = _conv_row_len(pl.program_id(0) * tm)
    y = _silu(_dwconv3_tile(x_ref[...].astype(F32), w_ref, row_len, tm))
    q_blocks = GD_QK // tn
    is_qk = j < 2 * q_blocks
    q_scale = jnp.where(j < q_blocks, GD_DK ** -0.5, 1.0)
    for hh in range(tn // GD_DK):
        yh = y[:, hh * GD_DK:(hh + 1) * GD_DK]
        r = lax.rsqrt(jnp.sum(yh * yh, axis=-1, keepdims=True) + EPS) * q_scale
        o_ref[:, hh * GD_DK:(hh + 1) * GD_DK] = (yh * jnp.where(is_qk, r, 1.0)).astype(o_ref.dtype)


def _gdn_prep(proj, w_conv):
    tm, tn = 512, 512
    return pl.pallas_call(
        functools.partial(_gdn_prep_kernel, tm=tm, tn=tn),
        grid=(T_ALL // tm, GD_CONV_CH // tn),
        in_specs=[
            pl.BlockSpec((tm, tn), lambda i, j: (i, j)),
            pl.BlockSpec((3, tn), lambda i, j: (0, j)),
        ],
        out_specs=pl.BlockSpec((tm, tn), lambda i, j: (i, j)),
        out_shape=jax.ShapeDtypeStruct((T_ALL, GD_CONV_CH), BF16),
        compiler_params=_cparams(2),
        name="gdn_conv_norm",
    )(proj, w_conv)


def _gdn_kernel(q_ref, k_ref, v_ref, gc_ref, gr_ref, pc_ref, pr_ref, s0_ref,
                o_ref, sout_ref, s_s):
    L = GD_CHUNK
    d = pl.program_id(0)
    j = GD_SLOTS.slot(d, pl.program_id(2))
    is_ctx, is_head, is_tail = GD_SLOTS.flags(d, j)

    @pl.when(jnp.logical_and(is_head, is_ctx))
    def _():
        s_s[...] = jnp.zeros_like(s_s)

    @pl.when(jnp.logical_and(is_head, jnp.logical_not(is_ctx)))
    def _():
        s_s[...] = s0_ref[...]

    incl, strict = GD_SLOTS.mask(d)
    tri = incl.astype(F32)
    nh = GD_GV
    gc = gc_ref[...]
    gr = gr_ref[...]
    g_c = -jnp.exp(pc_ref[0:1, :]) * _softplus(gc[:, 0:nh] + pc_ref[1:2, :])
    G_c = _dot(tri, g_c, precision=HIGHEST)
    beta_c = _sigmoid(gc[:, nh:2 * nh])
    g_r = -jnp.exp(pr_ref[:, 0:1]) * _softplus(gr[0:nh, :] + pr_ref[:, 1:2])
    G_r = _dot_nt(g_r, tri, precision=HIGHEST)
    Gl_all = jnp.where(d == 0, G_c[L - 1:L, :], G_c[0:1, :])

    rep = GD_V_HEADS // GD_QK_HEADS
    heads = range(nh)
    k32, qb, kk, qk = [], [], [], []
    for hq in range(GD_GQ):
        kb = k_ref[:, hq * GD_DK:(hq + 1) * GD_DK]
        k32.append(kb.astype(F32))
        qb.append(q_ref[:, hq * GD_DK:(hq + 1) * GD_DK])
        kk.append(_dot_nt(kb, kb))
        qk.append(_dot_nt(qb[hq], kb))

    decay, neg_a, rhs, eG = [], [], [], []
    for hv in heads:
        Gc = G_c[:, hv:hv + 1]
        beta = beta_c[:, hv:hv + 1]
        decay.append(jnp.where(incl, jnp.exp(jnp.where(incl, Gc - G_r[hv:hv + 1, :], 0.0)), 0.0))
        neg_a.append(jnp.where(strict, -(beta * kk[hv // rep] * decay[hv]), 0.0))
        eG.append(jnp.exp(Gc))
        v32 = v_ref[:, hv * GD_DV:(hv + 1) * GD_DV].astype(F32)
        rhs.append(jnp.concatenate([(beta * eG[hv]) * k32[hv // rep], beta * v32], axis=-1))

    n_hi = [a.astype(BF16) for a in neg_a]
    pw = n_hi
    m_inv = list(neg_a)
    for _ in range(5):
        pw32 = [_dot(p, p) for p in pw]
        pw = [p.astype(BF16) for p in pw32]
        m_inv = [m + p32 + _dot(m.astype(BF16), p) for m, p32, p in zip(m_inv, pw32, pw)]
    mb = [m.astype(BF16) for m in m_inv]
    x1 = [r + _dot(m, r.astype(BF16)) for m, r in zip(mb, rhs)]
    res = []
    for r, x, a, ah in zip(rhs, x1, neg_a, n_hi):
        al = (a - ah.astype(F32)).astype(BF16)
        xh = x.astype(BF16)
        xl = (x - xh.astype(F32)).astype(BF16)
        hi = _dot(jnp.concatenate([ah, al], axis=0), xh)
        res.append((r - x) + (hi[0:L] + hi[L:2 * L] + _dot(ah, xl)))
    sol = [x + r + _dot(m, r.astype(BF16)) for x, r, m in zip(x1, res, mb)]

    S = [s_s[hv] for hv in heads]
    Sb = [x.astype(BF16) for x in S]
    ws = [_dot(sol[hv][:, 0:GD_DK].astype(BF16), Sb[hv]) for hv in heads]
    qs = [_dot(qb[hv // rep], Sb[hv]) for hv in heads]
    ub = [(sol[hv][:, GD_DK:GD_DK + GD_DV] - ws[hv]).astype(BF16) for hv in heads]
    pu = [_dot((qk[hv // rep] * decay[hv]).astype(BF16), ub[hv]) for hv in heads]
    kdec = [(jnp.exp(Gl_all[:, hv:hv + 1] - G_c[:, hv:hv + 1]) * k32[hv // rep]).astype(BF16) for hv in heads]
    ku = [_dot_tn(kdec[hv], ub[hv]) for hv in heads]
    for hv in heads:
        o_ref[:, hv * GD_DV:(hv + 1) * GD_DV] = (eG[hv] * qs[hv] + pu[hv]).astype(o_ref.dtype)
        s_s[hv] = jnp.exp(Gl_all[:, hv:hv + 1]) * S[hv] + ku[hv]

    @pl.when(jnp.logical_and(is_tail, is_ctx))
    def _():
        sout_ref[...] = s_s[...]


def _gdn_scan(qkv, ab, a_log, dt_bias, cache_s, j_layer):
    HV, nh, L, S = GD_V_HEADS, GD_GV, GD_CHUNK, GD_SLOTS
    a_d = ab[:, 0:2 * HV].reshape(T_ALL, 2, GD_GROUPS, nh)
    b_d = ab[:, 2 * HV:4 * HV].reshape(T_ALL, 2, GD_GROUPS, nh)
    gcol = jnp.concatenate([a_d, b_d], axis=-1).reshape(S.n, L, 2, GD_GROUPS, 2 * nh)
    gcol = gcol.transpose(2, 3, 0, 1, 4)
    grow = gcol.transpose(0, 1, 2, 4, 3)
    par = jnp.stack([a_log.astype(F32), dt_bias.astype(F32)], axis=1)
    pcol = par.reshape(2, 2, GD_GROUPS, nh).transpose(0, 2, 1, 3)
    prow = pcol.transpose(0, 1, 3, 2)

    qb = GD_GQ * GD_DK
    vb = nh * GD_DV
    return pl.pallas_call(
        _gdn_kernel,
        grid=(2, GD_GROUPS, S.n),
        in_specs=[
            pl.BlockSpec((L, qb), lambda d, p, g: (S.slot(d, g), p)),
            pl.BlockSpec((L, qb), lambda d, p, g: (S.slot(d, g), GD_GROUPS + p)),
            pl.BlockSpec((L, vb), lambda d, p, g: (S.slot(d, g), GD_GROUPS + p)),
            pl.BlockSpec((None, None, None, L, 2 * nh), lambda d, p, g: (d, p, S.slot(d, g), 0, 0)),
            pl.BlockSpec((None, None, None, 2 * nh, L), lambda d, p, g: (d, p, S.slot(d, g), 0, 0)),
            pl.BlockSpec((None, None, 2, nh), lambda d, p, g: (d, p, 0, 0)),
            pl.BlockSpec((None, None, nh, 2), lambda d, p, g: (d, p, 0, 0)),
            pl.BlockSpec((None, None, None, nh, GD_DK, GD_DV),
                         lambda d, p, g: (S.lat_batch(d, g), j_layer, d, p, 0, 0)),
        ],
        out_specs=[
            pl.BlockSpec((None, L, vb), lambda d, p, g: (d, S.slot(d, g), p)),
            pl.BlockSpec((None, None, nh, GD_DK, GD_DV), lambda d, p, g: (S.ctx_seq(d, g), d, p, 0, 0)),
        ],
        out_shape=[
            jax.ShapeDtypeStruct((2, T_ALL, GD_INNER), BF16),
            jax.ShapeDtypeStruct((BATCH, 2, HV, GD_DK, GD_DV), F32),
        ],
        scratch_shapes=[pltpu.VMEM((nh, GD_DK, GD_DV), F32)],
        compiler_params=_cparams(3),
        name="gdn_scan",
    )(qkv, qkv, qkv, gcol, grow, pcol, prow, cache_s)


def _gdn_post_kernel(of_ref, ob_ref, z_ref, g_ref, y_ref, *, tn):
    g = g_ref[...]
    for hh in range(tn // GD_DV):
        sl = slice(hh * GD_DV, (hh + 1) * GD_DV)
        o = of_ref[:, sl].astype(F32) + ob_ref[:, sl].astype(F32)
        r = lax.rsqrt(jnp.mean(o * o, axis=-1, keepdims=True) + EPS)
        y_ref[:, sl] = ((o * r * g) * _silu(z_ref[:, sl].astype(F32))).astype(BF16)


def _gdn_post(odir, proj, g_norm):
    tm, tn = 512, 512
    z_blk = GD_CONV_CH // tn
    return pl.pallas_call(
        functools.partial(_gdn_post_kernel, tn=tn),
        grid=(T_ALL // tm, GD_INNER // tn),
        in_specs=[
            pl.BlockSpec((None, tm, tn), lambda i, j: (0, i, j)),
            pl.BlockSpec((None, tm, tn), lambda i, j: (1, i, j)),
            pl.BlockSpec((tm, tn), lambda i, j: (i, z_blk + j)),
            pl.BlockSpec((1, GD_DV), lambda i, j: (0, 0)),
        ],
        out_specs=pl.BlockSpec((tm, tn), lambda i, j: (i, j)),
        out_shape=jax.ShapeDtypeStruct((T_ALL, GD_INNER), BF16),
        compiler_params=_cparams(2),
        name="gdn_gate_norm",
    )(odir, odir, proj, g_norm.reshape(1, GD_DV))


def kernel(x_prompt, x_sample, c, cache_ml_C, cache_ml_n, cache_ml_m, cache_gd_S, c_ctx, w_ada, b_ada, g_norm,
           w_ml_in, b_ml_gate, g_ml_head, w_ml_out, w_sc_in, w_sc_conv, w_sc_out, w_gd_in, w_gd_conv, gd_A_log,
           gd_dt_bias, g_gd_norm, w_gd_out, g_final):
    x = jnp.concatenate([x_prompt.reshape(T_CTX, D_MODEL), x_sample.reshape(T_LAT, D_MODEL)], axis=0)
    cond = jnp.concatenate([c_ctx[None, :], c, jnp.zeros((N_COND - 1 - DEC_BATCH, D_MODEL), F32)], axis=0)
    mod = _modulation(cond, w_ada, b_ada)
    w_ml_nk = jnp.swapaxes(w_ml_in, 1, 2)

    state_ml_c = None
    ml_n, ml_m, gd_s = [], [], []
    for l in range(DEPTH):
        shift = mod[l, :, 0:D_MODEL]
        scale = mod[l, :, D_MODEL:2 * D_MODEL]
        gate = mod[l, :, 2 * D_MODEL:3 * D_MODEL]
        h = _norm_mod(x, g_norm[l], shift, scale)
        j = l // 3
        kind = l % 3
        if kind == 0:
            proj = _project(h, w_ml_nk, j, 0, ML_MAIN, 1024, BF16, w_is_nk=True)
            gates = _project(h, w_ml_nk, j, ML_MAIN, 4 * ML_HEADS, 4 * ML_HEADS, F32, w_is_nk=True)
            hdir, state_ml_c, n_fin, m_fin = _mlstm_scan(proj, gates, b_ml_gate[j], cache_ml_C, cache_ml_n,
                                                         cache_ml_m, j, state_ml_c)
            ml_n.append(n_fin)
            ml_m.append(m_fin[..., 0])
            y = _mlstm_post(hdir, proj, g_ml_head[j])
            x = _project_residual(y, w_ml_out, j, x, gate)
        elif kind == 1:
            proj = _project(h, w_sc_in, j, 0, 4 * SC_INNER, 1024, BF16)
            y = _sc_post(proj, w_sc_conv[j])
            x = _project_residual(y, w_sc_out, j, x, gate)
        else:
            proj = _project(h, w_gd_in, j, 0, GD_MAIN, 1024, BF16)
            ab = _project(h, w_gd_in, j, GD_MAIN, 4 * GD_V_HEADS, 4 * GD_V_HEADS, F32)
            qkv = _gdn_prep(proj, w_gd_conv[j])
            odir, s_fin = _gdn_scan(qkv, ab, gd_A_log[j], gd_dt_bias[j], cache_gd_S, j)
            gd_s.append(s_fin)
            y = _gdn_post(odir, proj, g_gd_norm[j])
            x = _project_residual(y, w_gd_out, j, x, gate)

    y_ctx, y_lat = _final_norm(x, g_final)
    y_prompt = y_ctx.reshape(BATCH, SEQ, D_MODEL)
    y_sample = y_lat.reshape(DEC_BATCH, DEC_SEQ, D_MODEL)
    state_ml_n = jnp.stack(ml_n, axis=1)
    state_ml_m = jnp.stack(ml_m, axis=1)
    state_gd_s = jnp.stack(gd_s, axis=1)
    return (y_prompt, y_sample, state_ml_c, state_ml_n, state_ml_m, state_gd_s)
```

```python
import functools

import jax
import jax.numpy as jnp
from jax import lax
from jax.experimental import pallas as pl
from jax.experimental.pallas import tpu as pltpu

F32 = jnp.float32
BF16 = jnp.bfloat16
HIGHEST = lax.Precision.HIGHEST

D_MODEL = 2048
BATCH = 16
SEQ = 256
DEPTH = 4
DEC_BATCH = 2
DEC_SEQ = 2048
GRID_W = 64
EPS = 1e-6

T_CTX = BATCH * SEQ
T_LAT = DEC_BATCH * DEC_SEQ
T_ALL = T_CTX + T_LAT
N_COND = 8

ML_HEADS = 8
ML_DK = 256
ML_DV = 512
ML_QK = ML_HEADS * ML_DK
ML_INNER = ML_HEADS * ML_DV
ML_MAIN = 2 * ML_QK + 3 * ML_INNER
N_ML = 2
ML_CHUNK = 256
ML_CBLK = 256

SC_INNER = 2 * D_MODEL

GD_DK = 128
GD_DV = 128
GD_QK_HEADS = 16
GD_V_HEADS = 32
GD_QK = GD_QK_HEADS * GD_DK
GD_INNER = GD_V_HEADS * GD_DV
GD_CONV_CH = 2 * GD_QK + GD_INNER
GD_MAIN = GD_CONV_CH + GD_INNER
GD_GROUPS = 1
GD_GV = GD_V_HEADS // GD_GROUPS
GD_GQ = GD_QK_HEADS // GD_GROUPS
GD_CHUNK = 64

LANE = 128
BF16_ROWS = 16
VMEM_LIMIT = 56 * 1024 * 1024


def _cparams(n_axes):
    return pltpu.CompilerParams(dimension_semantics=("arbitrary",) * n_axes, vmem_limit_bytes=VMEM_LIMIT)


def _sigmoid(x):
    return 0.5 * jnp.tanh(0.5 * x) + 0.5


def _silu(x):
    h = 0.5 * x
    return h + h * jnp.tanh(h)


def _softplus(x):
    return jnp.maximum(x, 0.0) + jnp.log1p(jnp.exp(-jnp.abs(x)))


def _log_sigmoid(x):
    return -_softplus(-x)


def _cond_group(row0):
    return jnp.where(row0 < T_CTX, 0, 1 + (row0 - T_CTX) // DEC_SEQ)


def _conv_row_len(row0):
    return jnp.where(row0 < T_CTX, SEQ, GRID_W)


def _dot_nt(a, b, **kw):
    return lax.dot_general(a, b, (((1,), (1,)), ((), ())), preferred_element_type=F32, **kw)


def _dot_tn(a, b, **kw):
    return lax.dot_general(a, b, (((0,), (0,)), ((), ())), preferred_element_type=F32, **kw)


def _dot(a, b, **kw):
    return jnp.dot(a, b, preferred_element_type=F32, **kw)


def _mod_kernel(s_ref, w_ref, b_ref, o_ref):
    a = _silu(s_ref[...]).astype(BF16)
    o_ref[...] = _dot(a, w_ref[...].astype(BF16)) + b_ref[...]


def _modulation(cond, w_ada, b_ada):
    tn = 1024
    n = 3 * D_MODEL
    return pl.pallas_call(
        _mod_kernel,
        grid=(DEPTH, n // tn),
        in_specs=[
            pl.BlockSpec((N_COND, D_MODEL), lambda l, j: (0, 0)),
            pl.BlockSpec((None, D_MODEL, tn), lambda l, j: (l, 0, j)),
            pl.BlockSpec((None, 1, tn), lambda l, j: (l, 0, j)),
        ],
        out_specs=pl.BlockSpec((None, N_COND, tn), lambda l, j: (l, 0, j)),
        out_shape=jax.ShapeDtypeStruct((DEPTH, N_COND, n), F32),
        compiler_params=_cparams(2),
        name="adaln_modulation",
    )(cond, w_ada, b_ada.reshape(DEPTH, 1, n))


def _norm_mod_kernel(x_ref, g_ref, shift_ref, scale_ref, o_ref, *, tm):
    grp = _cond_group(pl.program_id(0) * tm)
    x = x_ref[...]
    r = lax.rsqrt(jnp.mean(x * x, axis=-1, keepdims=True) + EPS)
    xn = x * r * g_ref[...]
    sh = shift_ref[pl.ds(grp, 1), :]
    sc = scale_ref[pl.ds(grp, 1), :]
    o_ref[...] = (xn * (1.0 + sc) + sh).astype(BF16)


def _norm_mod(x, g, shift, scale):
    tm = 512
    return pl.pallas_call(
        functools.partial(_norm_mod_kernel, tm=tm),
        grid=(T_ALL // tm,),
        in_specs=[
            pl.BlockSpec((tm, D_MODEL), lambda i: (i, 0)),
            pl.BlockSpec((1, D_MODEL), lambda i: (0, 0)),
            pl.BlockSpec((N_COND, D_MODEL), lambda i: (0, 0)),
            pl.BlockSpec((N_COND, D_MODEL), lambda i: (0, 0)),
        ],
        out_specs=pl.BlockSpec((tm, D_MODEL), lambda i: (i, 0)),
        out_shape=jax.ShapeDtypeStruct((T_ALL, D_MODEL), BF16),
        compiler_params=_cparams(1),
        name="norm_modulate",
    )(x, g.reshape(1, D_MODEL), shift, scale)


def _final_norm_kernel(x_ref, g_ref, ctx_ref, lat_ref, *, ctx_tiles):
    x = x_ref[...]
    r = lax.rsqrt(jnp.mean(x * x, axis=-1, keepdims=True) + EPS)
    y = x * r * g_ref[...]
    i = pl.program_id(0)

    @pl.when(i < ctx_tiles)
    def _():
        ctx_ref[...] = y

    @pl.when(i >= ctx_tiles)
    def _():
        lat_ref[...] = y


def _final_norm(x, g):
    tm = 512
    ctx_tiles = T_CTX // tm
    return pl.pallas_call(
        functools.partial(_final_norm_kernel, ctx_tiles=ctx_tiles),
        grid=(T_ALL // tm,),
        in_specs=[
            pl.BlockSpec((tm, D_MODEL), lambda i: (i, 0)),
            pl.BlockSpec((1, D_MODEL), lambda i: (0, 0)),
        ],
        out_specs=[
            pl.BlockSpec((tm, D_MODEL), lambda i: (jnp.minimum(i, ctx_tiles - 1), 0)),
            pl.BlockSpec((tm, D_MODEL), lambda i: (jnp.maximum(i - ctx_tiles, 0), 0)),
        ],
        out_shape=[
            jax.ShapeDtypeStruct((T_CTX, D_MODEL), F32),
            jax.ShapeDtypeStruct((T_LAT, D_MODEL), F32),
        ],
        compiler_params=_cparams(1),
        name="final_norm",
    )(x, g.reshape(1, D_MODEL))


def _proj_kernel(a_ref, w_ref, o_ref, wb_ref, *, w_is_nk):
    @pl.when(pl.program_id(1) == 0)
    def _():
        wb_ref[...] = w_ref[...].astype(BF16)

    dot = _dot_nt if w_is_nk else _dot
    o_ref[...] = dot(a_ref[...], wb_ref[...]).astype(o_ref.dtype)


def _project(a, w, layer, col0, n_cols, tn, out_dtype, w_is_nk=False, tm=1024):
    k = a.shape[1]
    j0 = col0 // tn
    if w_is_nk:
        w_spec = pl.BlockSpec((None, tn, k), lambda j, i: (layer, j0 + j, 0))
        w_tile = (tn, k)
    else:
        w_spec = pl.BlockSpec((None, k, tn), lambda j, i: (layer, 0, j0 + j))
        w_tile = (k, tn)
    return pl.pallas_call(
        functools.partial(_proj_kernel, w_is_nk=w_is_nk),
        grid=(n_cols // tn, T_ALL // tm),
        in_specs=[pl.BlockSpec((tm, k), lambda j, i: (i, 0)), w_spec],
        out_specs=pl.BlockSpec((tm, tn), lambda j, i: (i, j)),
        out_shape=jax.ShapeDtypeStruct((T_ALL, n_cols), out_dtype),
        scratch_shapes=[pltpu.VMEM(w_tile, BF16)],
        compiler_params=_cparams(2),
        name="in_projection",
    )(a, w)


def _proj_res_kernel(a_ref, w_ref, x_ref, gate_ref, o_ref, wb_ref, *, tm):
    i = pl.program_id(1)

    @pl.when(i == 0)
    def _():
        wb_ref[...] = w_ref[...].astype(BF16)

    gate = gate_ref[pl.ds(_cond_group(i * tm), 1), :]
    o_ref[...] = x_ref[...] + gate * _dot(a_ref[...], wb_ref[...])


def _project_residual(a, w, layer, x, gate):
    k = a.shape[1]
    tm, tn = 1024, 512
    return pl.pallas_call(
        functools.partial(_proj_res_kernel, tm=tm),
        grid=(D_MODEL // tn, T_ALL // tm),
        in_specs=[
            pl.BlockSpec((tm, k), lambda j, i: (i, 0)),
            pl.BlockSpec((None, k, tn), lambda j, i: (layer, 0, j)),
            pl.BlockSpec((tm, tn), lambda j, i: (i, j)),
            pl.BlockSpec((N_COND, tn), lambda j, i: (0, j)),
        ],
        out_specs=pl.BlockSpec((tm, tn), lambda j, i: (i, j)),
        out_shape=jax.ShapeDtypeStruct((T_ALL, D_MODEL), F32),
        scratch_shapes=[pltpu.VMEM((k, tn), BF16)],
        compiler_params=_cparams(2),
        name="out_projection_residual",
    )(a, w, x, gate)


class _Slots:
    def __init__(self, chunk):
        self.chunk = chunk
        self.n = T_ALL // chunk
        self.ctx = T_CTX // chunk
        self.ctx_chunks = SEQ // chunk
        self.lat_chunks = DEC_SEQ // chunk

    def slot(self, d, g):
        return g + d * (self.n - 1 - 2 * g)

    def seq(self, j):
        return jnp.where(j < self.ctx, j // self.ctx_chunks, BATCH + (j - self.ctx) // self.lat_chunks)

    def flags(self, d, j):
        is_ctx = j < self.ctx
        pos = jnp.where(is_ctx, j % self.ctx_chunks, (j - self.ctx) % self.lat_chunks)
        n_chunks = jnp.where(is_ctx, self.ctx_chunks, self.lat_chunks)
        head_pos = jnp.where(d == 0, 0, n_chunks - 1)
        tail_pos = jnp.where(d == 0, n_chunks - 1, 0)
        return is_ctx, pos == head_pos, pos == tail_pos

    def mask(self, d):
        t_i = lax.broadcasted_iota(jnp.int32, (self.chunk, self.chunk), 0)
        s_i = lax.broadcasted_iota(jnp.int32, (self.chunk, self.chunk), 1)
        diff = (t_i - s_i) * (1 - 2 * d)
        return diff >= 0, diff > 0

    def lat_batch(self, d, g):
        return jnp.clip(self.seq(self.slot(d, g)) - BATCH, 0, DEC_BATCH - 1)

    def ctx_seq(self, d, g):
        return jnp.minimum(self.seq(self.slot(d, g)), BATCH - 1)


ML_SLOTS = _Slots(ML_CHUNK)
GD_SLOTS = _Slots(GD_CHUNK)


def _mlstm_kernel(q_ref, k_ref, v_ref, gc_ref, gr_ref, bc_ref, br_ref, c0_ref, n0_ref, m0_ref, *rest,
                  has_acc):
    if has_acc:
        rest = rest[1:]
    h_ref, cout_ref, nout_ref, mout_ref, c_s, n_s, m_s = rest
    L = ML_CHUNK
    d = pl.program_id(0)
    j = ML_SLOTS.slot(d, pl.program_id(1))
    is_ctx, is_head, is_tail = ML_SLOTS.flags(d, j)

    @pl.when(jnp.logical_and(is_head, is_ctx))
    def _():
        c_s[...] = jnp.zeros_like(c_s)
        n_s[...] = jnp.zeros_like(n_s)
        m_s[...] = jnp.zeros_like(m_s)

    @pl.when(jnp.logical_and(is_head, jnp.logical_not(is_ctx)))
    def _():
        c_s[...] = c0_ref[...]
        n_s[...] = n0_ref[...]
        m_s[...] = m0_ref[...]

    mask, _ = ML_SLOTS.mask(d)
    tri = mask.astype(F32)
    gc = gc_ref[...] + bc_ref[...]
    gr = gr_ref[...] + br_ref[...]
    i_c = gc[:, 0:ML_HEADS]
    b_c = _dot(tri, _log_sigmoid(gc[:, ML_HEADS:2 * ML_HEADS]), precision=HIGHEST)
    i_r = gr[0:ML_HEADS, :]
    b_r = _dot_nt(_log_sigmoid(gr[ML_HEADS:2 * ML_HEADS, :]), tri, precision=HIGHEST)
    bl_all = jnp.where(d == 0, b_c[L - 1:L, :], b_c[0:1, :])

    a_all, m_new_all, dec_all = [], [], []
    qbs = [q_ref[:, h * ML_DK:(h + 1) * ML_DK] * (ML_DK ** -0.5) for h in range(ML_HEADS)]
    qkn = []
    for h in range(ML_HEADS):
        nb = jnp.broadcast_to(n_s[h:h + 1, :], (BF16_ROWS, ML_DK)).astype(BF16)
        qkn.append(_dot_nt(qbs[h], jnp.concatenate([k_ref[:, h * ML_DK:(h + 1) * ML_DK], nb], axis=0)))
    lhs = []
    for h in range(ML_HEADS):
        bcol = b_c[:, h:h + 1]
        icol = i_c[:, h:h + 1]
        m = m_s[h:h + 1, 0:1]
        z = jnp.where(mask, i_r[h:h + 1, :] - b_r[h:h + 1, :], -jnp.inf)
        inter = bcol + m
        mt = jnp.maximum(inter, bcol + jnp.max(z, axis=-1, keepdims=True))
        s = qkn[h][:, 0:L] * jnp.exp(z + (bcol - mt))
        sc = jnp.exp(inter - mt)
        den = sc * qkn[h][:, L:L + 1] + jnp.sum(s, axis=-1, keepdims=True)
        rinv = 1.0 / jnp.maximum(jnp.abs(den), jnp.exp(-mt))
        bl = bl_all[:, h:h + 1]
        a = bl - bcol + icol
        m_new = jnp.maximum(bl + m, jnp.max(a, axis=0, keepdims=True))
        a_all.append(a)
        m_new_all.append(m_new)
        dec_all.append(jnp.exp(bl + m - m_new))
        lhs.append(jnp.concatenate([(qbs[h].astype(F32) * (sc * rinv)).astype(BF16), (s * rinv).astype(BF16)], axis=1))
    for h in range(ML_HEADS):
        w = jnp.concatenate([c_s[h].astype(BF16), v_ref[:, h * ML_DV:(h + 1) * ML_DV]], axis=0)
        h_ref[:, h * ML_DV:(h + 1) * ML_DV] = _dot(lhs[h], w).astype(h_ref.dtype)

    for h in range(ML_HEADS):
        wk = jnp.exp(a_all[h] - m_new_all[h]) * k_ref[:, h * ML_DK:(h + 1) * ML_DK].astype(F32)
        wkb = wk.astype(BF16)
        dec = dec_all[h]
        for cb in range(ML_DV // ML_CBLK):
            sl = slice(h * ML_DV + cb * ML_CBLK, h * ML_DV + (cb + 1) * ML_CBLK)
            csl = slice(cb * ML_CBLK, (cb + 1) * ML_CBLK)
            c_s[h, :, csl] = dec * c_s[h, :, csl] + _dot_tn(wkb, v_ref[:, sl])
        n_s[h:h + 1, :] = dec * n_s[h:h + 1, :] + jnp.sum(wk, axis=0, keepdims=True)
        m_s[h:h + 1, :] = jnp.broadcast_to(m_new_all[h], (1, LANE))

    @pl.when(jnp.logical_and(is_tail, is_ctx))
    def _():
        cout_ref[...] = c_s[...]
        nout_ref[...] = n_s[...]
        mout_ref[...] = m_s[...]


def _mlstm_scan(proj, gates, b_gate, cache_c, cache_n, cache_m, j_layer, c_acc):
    H, L, S = ML_HEADS, ML_CHUNK, ML_SLOTS
    gd = jnp.stack([jnp.concatenate([gates[:, 0:H], gates[:, 2 * H:3 * H]], axis=1),
                    jnp.concatenate([gates[:, H:2 * H], gates[:, 3 * H:4 * H]], axis=1)], axis=0)
    gcol = gd.reshape(2, S.n, L, 2 * H)
    grow = gcol.transpose(0, 1, 3, 2)
    bd = jnp.stack([jnp.concatenate([b_gate[0:H], b_gate[2 * H:3 * H]]),
                    jnp.concatenate([b_gate[H:2 * H], b_gate[3 * H:4 * H]])], axis=0)
    bcol = bd.reshape(2, 1, 2 * H)
    brow = bd.reshape(2, 2 * H, 1)
    m0 = jnp.broadcast_to(cache_m[:, j_layer][..., None], (DEC_BATCH, 2, H, LANE))

    has_acc = c_acc is not None
    in_specs = [
        pl.BlockSpec((L, ML_QK), lambda d, g: (S.slot(d, g), 0)),
        pl.BlockSpec((L, ML_QK), lambda d, g: (S.slot(d, g), 1)),
        pl.BlockSpec((L, ML_INNER), lambda d, g: (S.slot(d, g), 1)),
        pl.BlockSpec((None, None, L, 2 * H), lambda d, g: (d, S.slot(d, g), 0, 0)),
        pl.BlockSpec((None, None, 2 * H, L), lambda d, g: (d, S.slot(d, g), 0, 0)),
        pl.BlockSpec((None, 1, 2 * H), lambda d, g: (d, 0, 0)),
        pl.BlockSpec((None, 2 * H, 1), lambda d, g: (d, 0, 0)),
        pl.BlockSpec((None, None, None, H, ML_DK, ML_DV), lambda d, g: (S.lat_batch(d, g), j_layer, d, 0, 0, 0)),
        pl.BlockSpec((None, None, None, H, ML_DK), lambda d, g: (S.lat_batch(d, g), j_layer, d, 0, 0)),
        pl.BlockSpec((None, None, H, LANE), lambda d, g: (S.lat_batch(d, g), d, 0, 0)),
    ]
    args = [proj, proj, proj, gcol, grow, bcol, brow, cache_c, cache_n, m0]
    aliases = {}
    if has_acc:
        in_specs.append(pl.BlockSpec(memory_space=pl.ANY))
        args.append(c_acc)
        aliases = {len(args) - 1: 1}

    return pl.pallas_call(
        functools.partial(_mlstm_kernel, has_acc=has_acc),
        grid=(2, S.n),
        in_specs=in_specs,
        out_specs=[
            pl.BlockSpec((None, L, ML_INNER), lambda d, g: (d, S.slot(d, g), 0)),
            pl.BlockSpec((None, None, None, H, ML_DK, ML_DV), lambda d, g: (S.ctx_seq(d, g), j_layer, d, 0, 0, 0)),
            pl.BlockSpec((None, None, H, ML_DK), lambda d, g: (S.ctx_seq(d, g), d, 0, 0)),
            pl.BlockSpec((None, None, H, LANE), lambda d, g: (S.ctx_seq(d, g), d, 0, 0)),
        ],
        out_shape=[
            jax.ShapeDtypeStruct((2, T_ALL, ML_INNER), BF16),
            jax.ShapeDtypeStruct((BATCH, N_ML, 2, H, ML_DK, ML_DV), F32),
            jax.ShapeDtypeStruct((BATCH, 2, H, ML_DK), F32),
            jax.ShapeDtypeStruct((BATCH, 2, H, LANE), F32),
        ],
        scratch_shapes=[
            pltpu.VMEM((H, ML_DK, ML_DV), F32),
            pltpu.VMEM((H, ML_DK), F32),
            pltpu.VMEM((H, LANE), F32),
        ],
        input_output_aliases=aliases,
        compiler_params=_cparams(2),
        name="mlstm_scan",
    )(*args)


def _mlstm_post_kernel(hf_ref, hb_ref, o_ref, z_ref, g_ref, y_ref):
    hs = hf_ref[...].astype(F32) + hb_ref[...].astype(F32)
    r = lax.rsqrt(jnp.mean(hs * hs, axis=-1, keepdims=True) + EPS)
    y = (hs * r * g_ref[...]) * _sigmoid(o_ref[...].astype(F32)) * _silu(z_ref[...].astype(F32))
    y_ref[...] = y.astype(BF16)


def _mlstm_post(hdir, proj, g_head):
    tm = 1024
    o_blk = (2 * ML_QK + ML_INNER) // ML_DV
    z_blk = (2 * ML_QK + 2 * ML_INNER) // ML_DV
    return pl.pallas_call(
        _mlstm_post_kernel,
        grid=(T_ALL // tm, ML_HEADS),
        in_specs=[
            pl.BlockSpec((None, tm, ML_DV), lambda i, h: (0, i, h)),
            pl.BlockSpec((None, tm, ML_DV), lambda i, h: (1, i, h)),
            pl.BlockSpec((tm, ML_DV), lambda i, h: (i, o_blk + h)),
            pl.BlockSpec((tm, ML_DV), lambda i, h: (i, z_blk + h)),
            pl.BlockSpec((1, ML_DV), lambda i, h: (0, h)),
        ],
        out_specs=pl.BlockSpec((tm, ML_DV), lambda i, h: (i, h)),
        out_shape=jax.ShapeDtypeStruct((T_ALL, ML_INNER), BF16),
        compiler_params=_cparams(2),
        name="mlstm_gate_norm",
    )(hdir, hdir, proj, proj, g_head.reshape(1, ML_INNER))


def _dwconv3_tile(x, w_ref, row_len, tm):
    r = lax.broadcasted_iota(jnp.int32, (tm, 1), 0) & (row_len - 1)
    prev = jnp.where(r == 0, 0.0, pltpu.roll(x, 1, axis=0))
    nxt = jnp.where(r == row_len - 1, 0.0, pltpu.roll(x, tm - 1, axis=0))
    return prev * w_ref[0:1, :] + x * w_ref[1:2, :] + nxt * w_ref[2:3, :]


def _sc_kernel(a_ref, wu_ref, wb_ref, wc_ref, wz_ref, cw_ref, y_ref, wbuf, *, tm):
    i = pl.program_id(1)

    @pl.when(i == 0)
    def _():
        for g, w_ref in enumerate((wu_ref, wb_ref, wc_ref, wz_ref)):
            wbuf[g] = w_ref[...].astype(BF16)

    a = a_ref[...]
    u, b, c, z = (_dot(a, wbuf[g]) for g in range(4))
    conv = _dwconv3_tile(c * u, cw_ref, _conv_row_len(i * tm), tm)
    y_ref[...] = (b * conv * _silu(z)).astype(BF16)


def _shortconv(a, w_in, layer, w_conv):
    tm, tn = 1024, 256
    k = a.shape[1]
    nb = SC_INNER // tn
    w_specs = [pl.BlockSpec((None, k, tn), functools.partial(lambda j, i, g: (layer, 0, g * nb + j), g=g))
               for g in range(4)]
    return pl.pallas_call(
        functools.partial(_sc_kernel, tm=tm),
        grid=(nb, T_ALL // tm),
        in_specs=[pl.BlockSpec((tm, k), lambda j, i: (i, 0)), *w_specs,
                  pl.BlockSpec((None, 3, tn), lambda j, i: (layer, 0, j))],
        out_specs=pl.BlockSpec((tm, tn), lambda j, i: (i, j)),
        out_shape=jax.ShapeDtypeStruct((T_ALL, SC_INNER), BF16),
        scratch_shapes=[pltpu.VMEM((4, k, tn), BF16)],
        compiler_params=_cparams(2),
        name="shortconv_mixer",
    )(a, w_in, w_in, w_in, w_in, w_conv)


def _gdn_prep_kernel(x_ref, w_ref, o_ref, *, tm, tn):
    j = pl.program_id(1)
    row_len = _conv_row_len(pl.program_id(0) * tm)
    y = _silu(_dwconv3_tile(x_ref[...].astype(F32), w_ref, row_len, tm))
    q_blocks = GD_QK // tn
    is_qk = j < 2 * q_blocks
    q_scale = jnp.where(j < q_blocks, GD_DK ** -0.5, 1.0)
    for hh in range(tn // GD_DK):
        yh = y[:, hh * GD_DK:(hh + 1) * GD_DK]
        r = lax.rsqrt(jnp.sum(yh * yh, axis=-1, keepdims=True) + EPS) * q_scale
        o_ref[:, hh * GD_DK:(hh + 1) * GD_DK] = (yh * jnp.where(is_qk, r, 1.0)).astype(o_ref.dtype)


def _gdn_prep(proj, w_conv):
    tm, tn = 512, 1024
    return pl.pallas_call(
        functools.partial(_gdn_prep_kernel, tm=tm, tn=tn),
        grid=(T_ALL // tm, GD_CONV_CH // tn),
        in_specs=[
            pl.BlockSpec((tm, tn), lambda i, j: (i, j)),
            pl.BlockSpec((3, tn), lambda i, j: (0, j)),
        ],
        out_specs=pl.BlockSpec((tm, tn), lambda i, j: (i, j)),
        out_shape=jax.ShapeDtypeStruct((T_ALL, GD_CONV_CH), BF16),
        compiler_params=_cparams(2),
        name="gdn_conv_norm",
    )(proj, w_conv)


def _gdn_kernel(q_ref, k_ref, v_ref, gc_ref, gr_ref, pc_ref, pr_ref, s0_ref,
                o_ref, sout_ref, s_s):
    L = GD_CHUNK
    d = pl.program_id(0)
    j = GD_SLOTS.slot(d, pl.program_id(2))
    is_ctx, is_head, is_tail = GD_SLOTS.flags(d, j)

    @pl.when(jnp.logical_and(is_head, is_ctx))
    def _():
        s_s[...] = jnp.zeros_like(s_s)

    @pl.when(jnp.logical_and(is_head, jnp.logical_not(is_ctx)))
    def _():
        s_s[...] = s0_ref[...]

    incl, strict = GD_SLOTS.mask(d)
    tri = incl.astype(F32)
    nh = GD_GV
    gc = gc_ref[...]
    gr = gr_ref[...]
    g_c = -jnp.exp(pc_ref[0:1, :]) * _softplus(gc[:, 0:nh] + pc_ref[1:2, :])
    G_c = _dot(tri, g_c, precision=HIGHEST)
    beta_c = _sigmoid(gc[:, nh:2 * nh])
    g_r = -jnp.exp(pr_ref[:, 0:1]) * _softplus(gr[0:nh, :] + pr_ref[:, 1:2])
    G_r = _dot_nt(g_r, tri, precision=HIGHEST)
    Gl_all = jnp.where(d == 0, G_c[L - 1:L, :], G_c[0:1, :])

    rep = GD_V_HEADS // GD_QK_HEADS
    heads = range(nh)
    k32, qb, kk, qk = [], [], [], []
    for hq in range(GD_GQ):
        kb = k_ref[:, hq * GD_DK:(hq + 1) * GD_DK]
        k32.append(kb.astype(F32))
        qb.append(q_ref[:, hq * GD_DK:(hq + 1) * GD_DK])
        both = _dot_nt(jnp.concatenate([kb, qb[hq]], axis=0), kb)
        kk.append(both[0:L])
        qk.append(both[L:2 * L])

    decay, neg_a, rhs, eG = [], [], [], []
    for hv in heads:
        Gc = G_c[:, hv:hv + 1]
        beta = beta_c[:, hv:hv + 1]
        decay.append(jnp.where(incl, jnp.exp(jnp.where(incl, Gc - G_r[hv:hv + 1, :], 0.0)), 0.0))
        neg_a.append(jnp.where(strict, -(beta * kk[hv // rep] * decay[hv]), 0.0))
        eG.append(jnp.exp(Gc))
        v32 = v_ref[:, hv * GD_DV:(hv + 1) * GD_DV].astype(F32)
        rhs.append(jnp.concatenate([(beta * eG[hv]) * k32[hv // rep], beta * v32], axis=-1))

    n_hi = [a.astype(BF16) for a in neg_a]
    m_inv = list(neg_a)
    p32 = [_dot(p, p) for p in n_hi]
    for k in range(1, 6):
        pw = [x.astype(BF16) for x in p32]
        if k < 5:
            both = [_dot(jnp.concatenate([p, m.astype(BF16)], axis=0), p) for p, m in zip(pw, m_inv)]
            m_inv = [m + x + b[L:2 * L] for m, x, b in zip(m_inv, p32, both)]
            p32 = [b[0:L] for b in both]
        else:
            m_inv = [m + x + _dot(m.astype(BF16), p) for m, x, p in zip(m_inv, p32, pw)]
    mb = [m.astype(BF16) for m in m_inv]
    x1 = [r + _dot(m, r.astype(BF16)) for m, r in zip(mb, rhs)]
    res = []
    for r, x, a, ah in zip(rhs, x1, neg_a, n_hi):
        al = (a - ah.astype(F32)).astype(BF16)
        xh = x.astype(BF16)
        xl = (x - xh.astype(F32)).astype(BF16)
        hi = _dot(jnp.concatenate([ah, al], axis=0), xh)
        res.append((r - x) + (hi[0:L] + hi[L:2 * L] + _dot(ah, xl)))
    sol = [x + r + _dot(m, r.astype(BF16)) for x, r, m in zip(x1, res, mb)]

    S = [s_s[hv] for hv in heads]
    Sb = [x.astype(BF16) for x in S]
    wq = [_dot(jnp.concatenate([sol[hv][:, 0:GD_DK].astype(BF16), qb[hv // rep]], axis=0), Sb[hv]) for hv in heads]
    qs = [x[L:2 * L] for x in wq]
    ub = [(sol[hv][:, GD_DK:GD_DK + GD_DV] - wq[hv][0:L]).astype(BF16) for hv in heads]
    pu = [_dot((qk[hv // rep] * decay[hv]).astype(BF16), ub[hv]) for hv in heads]
    kdec = [(jnp.exp(Gl_all[:, hv:hv + 1] - G_c[:, hv:hv + 1]) * k32[hv // rep]).astype(BF16) for hv in heads]
    ku = [_dot_tn(kdec[hv], ub[hv]) for hv in heads]
    for hv in heads:
        o_ref[:, hv * GD_DV:(hv + 1) * GD_DV] = (eG[hv] * qs[hv] + pu[hv]).astype(o_ref.dtype)
        s_s[hv] = jnp.exp(Gl_all[:, hv:hv + 1]) * S[hv] + ku[hv]

    @pl.when(jnp.logical_and(is_tail, is_ctx))
    def _():
        sout_ref[...] = s_s[...]


def _gdn_scan(qkv, ab, a_log, dt_bias, cache_s, j_layer):
    HV, nh, L, S = GD_V_HEADS, GD_GV, GD_CHUNK, GD_SLOTS
    a_d = ab[:, 0:2 * HV].reshape(T_ALL, 2, GD_GROUPS, nh)
    b_d = ab[:, 2 * HV:4 * HV].reshape(T_ALL, 2, GD_GROUPS, nh)
    gcol = jnp.concatenate([a_d, b_d], axis=-1).reshape(S.n, L, 2, GD_GROUPS, 2 * nh)
    gcol = gcol.transpose(2, 3, 0, 1, 4)
    grow = gcol.transpose(0, 1, 2, 4, 3)
    par = jnp.stack([a_log.astype(F32), dt_bias.astype(F32)], axis=1)
    pcol = par.reshape(2, 2, GD_GROUPS, nh).transpose(0, 2, 1, 3)
    prow = pcol.transpose(0, 1, 3, 2)

    qb = GD_GQ * GD_DK
    vb = nh * GD_DV
    return pl.pallas_call(
        _gdn_kernel,
        grid=(2, GD_GROUPS, S.n),
        in_specs=[
            pl.BlockSpec((L, qb), lambda d, p, g: (S.slot(d, g), p)),
            pl.BlockSpec((L, qb), lambda d, p, g: (S.slot(d, g), GD_GROUPS + p)),
            pl.BlockSpec((L, vb), lambda d, p, g: (S.slot(d, g), GD_GROUPS + p)),
            pl.BlockSpec((None, None, None, L, 2 * nh), lambda d, p, g: (d, p, S.slot(d, g), 0, 0)),
            pl.BlockSpec((None, None, None, 2 * nh, L), lambda d, p, g: (d, p, S.slot(d, g), 0, 0)),
            pl.BlockSpec((None, None, 2, nh), lambda d, p, g: (d, p, 0, 0)),
            pl.BlockSpec((None, None, nh, 2), lambda d, p, g: (d, p, 0, 0)),
            pl.BlockSpec((None, None, None, nh, GD_DK, GD_DV),
                         lambda d, p, g: (S.lat_batch(d, g), j_layer, d, p, 0, 0)),
        ],
        out_specs=[
            pl.BlockSpec((None, L, vb), lambda d, p, g: (d, S.slot(d, g), p)),
            pl.BlockSpec((None, None, nh, GD_DK, GD_DV), lambda d, p, g: (S.ctx_seq(d, g), d, p, 0, 0)),
        ],
        out_shape=[
            jax.ShapeDtypeStruct((2, T_ALL, GD_INNER), BF16),
            jax.ShapeDtypeStruct((BATCH, 2, HV, GD_DK, GD_DV), F32),
        ],
        scratch_shapes=[pltpu.VMEM((nh, GD_DK, GD_DV), F32)],
        compiler_params=_cparams(3),
        name="gdn_scan",
    )(qkv, qkv, qkv, gcol, grow, pcol, prow, cache_s)


def _gdn_post_kernel(of_ref, ob_ref, z_ref, g_ref, y_ref, *, tn):
    g = g_ref[...]
    for hh in range(tn // GD_DV):
        sl = slice(hh * GD_DV, (hh + 1) * GD_DV)
        o = of_ref[:, sl].astype(F32) + ob_ref[:, sl].astype(F32)
        r = lax.rsqrt(jnp.mean(o * o, axis=-1, keepdims=True) + EPS)
        y_ref[:, sl] = ((o * r * g) * _silu(z_ref[:, sl].astype(F32))).astype(BF16)


def _gdn_post(odir, proj, g_norm):
    tm, tn = 1024, 512
    z_blk = GD_CONV_CH // tn
    return pl.pallas_call(
        functools.partial(_gdn_post_kernel, tn=tn),
        grid=(T_ALL // tm, GD_INNER // tn),
        in_specs=[
            pl.BlockSpec((None, tm, tn), lambda i, j: (0, i, j)),
            pl.BlockSpec((None, tm, tn), lambda i, j: (1, i, j)),
            pl.BlockSpec((tm, tn), lambda i, j: (i, z_blk + j)),
            pl.BlockSpec((1, GD_DV), lambda i, j: (0, 0)),
        ],
        out_specs=pl.BlockSpec((tm, tn), lambda i, j: (i, j)),
        out_shape=jax.ShapeDtypeStruct((T_ALL, GD_INNER), BF16),
        compiler_params=_cparams(2),
        name="gdn_gate_norm",
    )(odir, odir, proj, g_norm.reshape(1, GD_DV))


def kernel(x_prompt, x_sample, c, cache_ml_C, cache_ml_n, cache_ml_m, cache_gd_S, c_ctx, w_ada, b_ada, g_norm,
           w_ml_in, b_ml_gate, g_ml_head, w_ml_out, w_sc_in, w_sc_conv, w_sc_out, w_gd_in, w_gd_conv, gd_A_log,
           gd_dt_bias, g_gd_norm, w_gd_out, g_final):
    x = jnp.concatenate([x_prompt.reshape(T_CTX, D_MODEL), x_sample.reshape(T_LAT, D_MODEL)], axis=0)
    cond = jnp.concatenate([c_ctx[None, :], c, jnp.zeros((N_COND - 1 - DEC_BATCH, D_MODEL), F32)], axis=0)
    mod = _modulation(cond, w_ada, b_ada)
    w_ml_nk = jnp.swapaxes(w_ml_in, 1, 2)

    state_ml_c = None
    ml_n, ml_m, gd_s = [], [], []
    for l in range(DEPTH):
        shift = mod[l, :, 0:D_MODEL]
        scale = mod[l, :, D_MODEL:2 * D_MODEL]
        gate = mod[l, :, 2 * D_MODEL:3 * D_MODEL]
        h = _norm_mod(x, g_norm[l], shift, scale)
        j = l // 3
        kind = l % 3
        if kind == 0:
            proj = _project(h, w_ml_nk, j, 0, ML_MAIN, 1024, BF16, w_is_nk=True)
            gates = _project(h, w_ml_nk, j, ML_MAIN, 4 * ML_HEADS, 4 * ML_HEADS, F32, w_is_nk=True)
            hdir, state_ml_c, n_fin, m_fin = _mlstm_scan(proj, gates, b_ml_gate[j], cache_ml_C, cache_ml_n,
                                                         cache_ml_m, j, state_ml_c)
            ml_n.append(n_fin)
            ml_m.append(m_fin[..., 0])
            y = _mlstm_post(hdir, proj, g_ml_head[j])
            x = _project_residual(y, w_ml_out, j, x, gate)
        elif kind == 1:
            y = _shortconv(h, w_sc_in, j, w_sc_conv)
            x = _project_residual(y, w_sc_out, j, x, gate)
        else:
            proj = _project(h, w_gd_in, j, 0, GD_MAIN, 1024, BF16)
            ab = _project(h, w_gd_in, j, GD_MAIN, 4 * GD_V_HEADS, 4 * GD_V_HEADS, F32)
            qkv = _gdn_prep(proj, w_gd_conv[j])
            odir, s_fin = _gdn_scan(qkv, ab, gd_A_log[j], gd_dt_bias[j], cache_gd_S, j)
            gd_s.append(s_fin)
            y = _gdn_post(odir, proj, g_gd_norm[j])
            x = _project_residual(y, w_gd_out, j, x, gate)

    y_ctx, y_lat = _final_norm(x, g_final)
    y_prompt = y_ctx.reshape(BATCH, SEQ, D_MODEL)
    y_sample = y_lat.reshape(DEC_BATCH, DEC_SEQ, D_MODEL)
    state_ml_n = jnp.stack(ml_n, axis=1)
    state_ml_m = jnp.stack(ml_m, axis=1)
    state_gd_s = jnp.stack(gd_s, axis=1)
    return (y_prompt, y_sample, state_ml_c, state_ml_n, state_ml_m, state_gd_s)
```

```python
import functools

import jax
import jax.numpy as jnp
from jax import lax
from jax.experimental import pallas as pl
from jax.experimental.pallas import tpu as pltpu

F32 = jnp.float32
BF16 = jnp.bfloat16
HIGHEST = lax.Precision.HIGHEST

D_MODEL = 2048
BATCH = 16
SEQ = 256
DEPTH = 4
DEC_BATCH = 2
DEC_SEQ = 2048
GRID_W = 64
EPS = 1e-6

T_CTX = BATCH * SEQ
T_LAT = DEC_BATCH * DEC_SEQ
T_ALL = T_CTX + T_LAT
N_COND = 8

ML_HEADS = 8
ML_DK = 256
ML_DV = 512
ML_QK = ML_HEADS * ML_DK
ML_INNER = ML_HEADS * ML_DV
ML_MAIN = 2 * ML_QK + 3 * ML_INNER
N_ML = 2
ML_CHUNK = 256
ML_CBLK = 256

SC_INNER = 2 * D_MODEL

GD_DK = 128
GD_DV = 128
GD_QK_HEADS = 16
GD_V_HEADS = 32
GD_QK = GD_QK_HEADS * GD_DK
GD_INNER = GD_V_HEADS * GD_DV
GD_CONV_CH = 2 * GD_QK + GD_INNER
GD_MAIN = GD_CONV_CH + GD_INNER
GD_GROUPS = 1
GD_GV = GD_V_HEADS // GD_GROUPS
GD_GQ = GD_QK_HEADS // GD_GROUPS
GD_CHUNK = 64

LANE = 128
BF16_ROWS = 16
VMEM_LIMIT = 56 * 1024 * 1024


def _cparams(n_axes):
    return pltpu.CompilerParams(dimension_semantics=("arbitrary",) * n_axes, vmem_limit_bytes=VMEM_LIMIT)


def _sigmoid(x):
    return 0.5 * jnp.tanh(0.5 * x) + 0.5


def _silu(x):
    h = 0.5 * x
    return h + h * jnp.tanh(h)


def _softplus(x):
    return jnp.maximum(x, 0.0) + jnp.log1p(jnp.exp(-jnp.abs(x)))


def _log_sigmoid(x):
    return -_softplus(-x)


def _cond_group(row0):
    return jnp.where(row0 < T_CTX, 0, 1 + (row0 - T_CTX) // DEC_SEQ)


def _conv_row_len(row0):
    return jnp.where(row0 < T_CTX, SEQ, GRID_W)


def _dot_nt(a, b, **kw):
    return lax.dot_general(a, b, (((1,), (1,)), ((), ())), preferred_element_type=F32, **kw)


def _dot_tn(a, b, **kw):
    return lax.dot_general(a, b, (((0,), (0,)), ((), ())), preferred_element_type=F32, **kw)


def _dot(a, b, **kw):
    return jnp.dot(a, b, preferred_element_type=F32, **kw)


def _mod_kernel(s_ref, w_ref, b_ref, o_ref):
    a = _silu(s_ref[...]).astype(BF16)
    o_ref[...] = _dot(a, w_ref[...].astype(BF16)) + b_ref[...]


def _modulation(cond, w_ada, b_ada):
    tn = 1024
    n = 3 * D_MODEL
    return pl.pallas_call(
        _mod_kernel,
        grid=(DEPTH, n // tn),
        in_specs=[
            pl.BlockSpec((N_COND, D_MODEL), lambda l, j: (0, 0)),
            pl.BlockSpec((None, D_MODEL, tn), lambda l, j: (l, 0, j)),
            pl.BlockSpec((None, 1, tn), lambda l, j: (l, 0, j)),
        ],
        out_specs=pl.BlockSpec((None, N_COND, tn), lambda l, j: (l, 0, j)),
        out_shape=jax.ShapeDtypeStruct((DEPTH, N_COND, n), F32),
        compiler_params=_cparams(2),
        name="adaln_modulation",
    )(cond, w_ada, b_ada.reshape(DEPTH, 1, n))


def _split_specs(block, index_map, tm):
    n_ctx = T_CTX // tm

    def ctx_map(*idx):
        row, *rest = index_map(*idx)
        return (jnp.minimum(row, n_ctx - 1), *rest)

    def lat_map(*idx):
        row, *rest = index_map(*idx)
        return (jnp.maximum(row - n_ctx, 0), *rest)

    return [pl.BlockSpec(block, ctx_map), pl.BlockSpec(block, lat_map)]


def _token_tile(refs, row0):
    if len(refs) == 1:
        return refs[0][...]
    return jnp.where(row0 < T_CTX, refs[0][...], refs[1][...])


def _norm_mod_kernel(*refs, tm):
    *x_refs, g_ref, shift_ref, scale_ref, o_ref = refs
    row0 = pl.program_id(0) * tm
    grp = _cond_group(row0)
    x = _token_tile(x_refs, row0)
    r = lax.rsqrt(jnp.mean(x * x, axis=-1, keepdims=True) + EPS)
    xn = x * r * g_ref[...]
    sh = shift_ref[pl.ds(grp, 1), :]
    sc = scale_ref[pl.ds(grp, 1), :]
    o_ref[...] = (xn * (1.0 + sc) + sh).astype(BF16)


def _norm_mod(xs, g, shift, scale):
    tm = 512
    x_map = lambda i: (i, 0)
    x_specs = [pl.BlockSpec((tm, D_MODEL), x_map)] if len(xs) == 1 else _split_specs((tm, D_MODEL), x_map, tm)
    return pl.pallas_call(
        functools.partial(_norm_mod_kernel, tm=tm),
        grid=(T_ALL // tm,),
        in_specs=[
            *x_specs,
            pl.BlockSpec((1, D_MODEL), lambda i: (0, 0)),
            pl.BlockSpec((N_COND, D_MODEL), lambda i: (0, 0)),
            pl.BlockSpec((N_COND, D_MODEL), lambda i: (0, 0)),
        ],
        out_specs=pl.BlockSpec((tm, D_MODEL), lambda i: (i, 0)),
        out_shape=jax.ShapeDtypeStruct((T_ALL, D_MODEL), BF16),
        compiler_params=_cparams(1),
        name="norm_modulate",
    )(*xs, g.reshape(1, D_MODEL), shift, scale)


def _final_norm_kernel(x_ref, g_ref, ctx_ref, lat_ref, *, ctx_tiles):
    x = x_ref[...]
    r = lax.rsqrt(jnp.mean(x * x, axis=-1, keepdims=True) + EPS)
    y = x * r * g_ref[...]
    i = pl.program_id(0)

    @pl.when(i < ctx_tiles)
    def _():
        ctx_ref[...] = y

    @pl.when(i >= ctx_tiles)
    def _():
        lat_ref[...] = y


def _final_norm(x, g):
    tm = 512
    ctx_tiles = T_CTX // tm
    return pl.pallas_call(
        functools.partial(_final_norm_kernel, ctx_tiles=ctx_tiles),
        grid=(T_ALL // tm,),
        in_specs=[
            pl.BlockSpec((tm, D_MODEL), lambda i: (i, 0)),
            pl.BlockSpec((1, D_MODEL), lambda i: (0, 0)),
        ],
        out_specs=[
            pl.BlockSpec((tm, D_MODEL), lambda i: (jnp.minimum(i, ctx_tiles - 1), 0)),
            pl.BlockSpec((tm, D_MODEL), lambda i: (jnp.maximum(i - ctx_tiles, 0), 0)),
        ],
        out_shape=[
            jax.ShapeDtypeStruct((T_CTX, D_MODEL), F32),
            jax.ShapeDtypeStruct((T_LAT, D_MODEL), F32),
        ],
        compiler_params=_cparams(1),
        name="final_norm",
    )(x, g.reshape(1, D_MODEL))


def _proj_kernel(a_ref, w_ref, o_ref, wb_ref, *, w_is_nk):
    @pl.when(pl.program_id(1) == 0)
    def _():
        wb_ref[...] = w_ref[...].astype(BF16)

    dot = _dot_nt if w_is_nk else _dot
    o_ref[...] = dot(a_ref[...], wb_ref[...]).astype(o_ref.dtype)


def _project(a, w, layer, col0, n_cols, tn, out_dtype, w_is_nk=False, tm=1024):
    k = a.shape[1]
    j0 = col0 // tn
    if w_is_nk:
        w_spec = pl.BlockSpec((None, tn, k), lambda j, i: (layer, j0 + j, 0))
        w_tile = (tn, k)
    else:
        w_spec = pl.BlockSpec((None, k, tn), lambda j, i: (layer, 0, j0 + j))
        w_tile = (k, tn)
    return pl.pallas_call(
        functools.partial(_proj_kernel, w_is_nk=w_is_nk),
        grid=(n_cols // tn, T_ALL // tm),
        in_specs=[pl.BlockSpec((tm, k), lambda j, i: (i, 0)), w_spec],
        out_specs=pl.BlockSpec((tm, tn), lambda j, i: (i, j)),
        out_shape=jax.ShapeDtypeStruct((T_ALL, n_cols), out_dtype),
        scratch_shapes=[pltpu.VMEM(w_tile, BF16)],
        compiler_params=_cparams(2),
        name="in_projection",
    )(a, w)


def _proj_res_kernel(a_ref, w_ref, *refs, tm):
    *x_refs, gate_ref, o_ref, wb_ref = refs
    i = pl.program_id(1)

    @pl.when(i == 0)
    def _():
        wb_ref[...] = w_ref[...].astype(BF16)

    gate = gate_ref[pl.ds(_cond_group(i * tm), 1), :]
    o_ref[...] = _token_tile(x_refs, i * tm) + gate * _dot(a_ref[...], wb_ref[...])


def _project_residual(a, w, layer, xs, gate):
    k = a.shape[1]
    tm, tn = 1024, 512
    x_map = lambda j, i: (i, j)
    x_specs = [pl.BlockSpec((tm, tn), x_map)] if len(xs) == 1 else _split_specs((tm, tn), x_map, tm)
    return pl.pallas_call(
        functools.partial(_proj_res_kernel, tm=tm),
        grid=(D_MODEL // tn, T_ALL // tm),
        in_specs=[
            pl.BlockSpec((tm, k), lambda j, i: (i, 0)),
            pl.BlockSpec((None, k, tn), lambda j, i: (layer, 0, j)),
            *x_specs,
            pl.BlockSpec((N_COND, tn), lambda j, i: (0, j)),
        ],
        out_specs=pl.BlockSpec((tm, tn), lambda j, i: (i, j)),
        out_shape=jax.ShapeDtypeStruct((T_ALL, D_MODEL), F32),
        scratch_shapes=[pltpu.VMEM((k, tn), BF16)],
        compiler_params=_cparams(2),
        name="out_projection_residual",
    )(a, w, *xs, gate)


class _Slots:
    def __init__(self, chunk):
        self.chunk = chunk
        self.n = T_ALL // chunk
        self.ctx = T_CTX // chunk
        self.ctx_chunks = SEQ // chunk
        self.lat_chunks = DEC_SEQ // chunk

    def slot(self, d, g):
        return g + d * (self.n - 1 - 2 * g)

    def seq(self, j):
        return jnp.where(j < self.ctx, j // self.ctx_chunks, BATCH + (j - self.ctx) // self.lat_chunks)

    def flags(self, d, j):
        is_ctx = j < self.ctx
        pos = jnp.where(is_ctx, j % self.ctx_chunks, (j - self.ctx) % self.lat_chunks)
        n_chunks = jnp.where(is_ctx, self.ctx_chunks, self.lat_chunks)
        head_pos = jnp.where(d == 0, 0, n_chunks - 1)
        tail_pos = jnp.where(d == 0, n_chunks - 1, 0)
        return is_ctx, pos == head_pos, pos == tail_pos

    def mask(self, d):
        t_i = lax.broadcasted_iota(jnp.int32, (self.chunk, self.chunk), 0)
        s_i = lax.broadcasted_iota(jnp.int32, (self.chunk, self.chunk), 1)
        diff = (t_i - s_i) * (1 - 2 * d)
        return diff >= 0, diff > 0

    def lat_batch(self, d, g):
        return jnp.clip(self.seq(self.slot(d, g)) - BATCH, 0, DEC_BATCH - 1)

    def ctx_seq(self, d, g):
        return jnp.minimum(self.seq(self.slot(d, g)), BATCH - 1)


ML_SLOTS = _Slots(ML_CHUNK)
GD_SLOTS = _Slots(GD_CHUNK)


def _mlstm_kernel(q_ref, k_ref, v_ref, gc_ref, gr_ref, bc_ref, br_ref, c0_ref, n0_ref, m0_ref, *rest,
                  has_acc):
    if has_acc:
        rest = rest[1:]
    h_ref, cout_ref, nout_ref, mout_ref, c_s, n_s, m_s = rest
    L = ML_CHUNK
    d = pl.program_id(0)
    j = ML_SLOTS.slot(d, pl.program_id(1))
    is_ctx, is_head, is_tail = ML_SLOTS.flags(d, j)

    @pl.when(jnp.logical_and(is_head, is_ctx))
    def _():
        c_s[...] = jnp.zeros_like(c_s)
        n_s[...] = jnp.zeros_like(n_s)
        m_s[...] = jnp.zeros_like(m_s)

    @pl.when(jnp.logical_and(is_head, jnp.logical_not(is_ctx)))
    def _():
        c_s[...] = c0_ref[...]
        n_s[...] = n0_ref[...]
        m_s[...] = m0_ref[...]

    mask, _ = ML_SLOTS.mask(d)
    tri = mask.astype(F32)
    gc = gc_ref[...] + bc_ref[...]
    gr = gr_ref[...] + br_ref[...]
    i_c = gc[:, 0:ML_HEADS]
    b_c = _dot(tri, _log_sigmoid(gc[:, ML_HEADS:2 * ML_HEADS]), precision=HIGHEST)
    i_r = gr[0:ML_HEADS, :]
    b_r = _dot_nt(_log_sigmoid(gr[ML_HEADS:2 * ML_HEADS, :]), tri, precision=HIGHEST)
    bl_all = jnp.where(d == 0, b_c[L - 1:L, :], b_c[0:1, :])

    a_all, m_new_all, dec_all = [], [], []
    qbs = [q_ref[:, h * ML_DK:(h + 1) * ML_DK] * (ML_DK ** -0.5) for h in range(ML_HEADS)]
    z_all, mt_all, sc_all = [], [], []
    for h in range(ML_HEADS):
        bcol = b_c[:, h:h + 1]
        m = m_s[h:h + 1, 0:1]
        z = jnp.where(mask, i_r[h:h + 1, :] - b_r[h:h + 1, :], -jnp.inf)
        inter = bcol + m
        mt = jnp.maximum(inter, bcol + jnp.max(z, axis=-1, keepdims=True))
        z_all.append(bcol - mt)
        mt_all.append(mt)
        sc_all.append(jnp.exp(inter - mt))
        bl = bl_all[:, h:h + 1]
        a = bl - bcol + i_c[:, h:h + 1]
        m_new = jnp.maximum(bl + m, jnp.max(a, axis=0, keepdims=True))
        a_all.append(a)
        m_new_all.append(m_new)
        dec_all.append(jnp.exp(bl + m - m_new))
    qkn = []
    for h in range(ML_HEADS):
        nb = jnp.broadcast_to(n_s[h:h + 1, :], (BF16_ROWS, ML_DK)).astype(BF16)
        qkn.append(_dot_nt(qbs[h], jnp.concatenate([k_ref[:, h * ML_DK:(h + 1) * ML_DK], nb], axis=0)))
    lhs, rinv = [], []
    for h in range(ML_HEADS):
        z = jnp.where(mask, i_r[h:h + 1, :] - b_r[h:h + 1, :], -jnp.inf)
        s = qkn[h][:, 0:L] * jnp.exp(z + z_all[h])
        den = sc_all[h] * qkn[h][:, L:L + 1] + jnp.sum(s, axis=-1, keepdims=True)
        rinv.append(1.0 / jnp.maximum(jnp.abs(den), jnp.exp(-mt_all[h])))
        lhs.append(jnp.concatenate([(qbs[h].astype(F32) * sc_all[h]).astype(BF16), s.astype(BF16)], axis=1))
    for h in range(ML_HEADS):
        w = jnp.concatenate([c_s[h].astype(BF16), v_ref[:, h * ML_DV:(h + 1) * ML_DV]], axis=0)
        h_ref[:, h * ML_DV:(h + 1) * ML_DV] = (_dot(lhs[h], w) * rinv[h]).astype(h_ref.dtype)

    for h in range(ML_HEADS):
        wk = jnp.exp(a_all[h] - m_new_all[h]) * k_ref[:, h * ML_DK:(h + 1) * ML_DK].astype(F32)
        wkb = wk.astype(BF16)
        dec = dec_all[h]
        for cb in range(ML_DV // ML_CBLK):
            sl = slice(h * ML_DV + cb * ML_CBLK, h * ML_DV + (cb + 1) * ML_CBLK)
            csl = slice(cb * ML_CBLK, (cb + 1) * ML_CBLK)
            c_s[h, :, csl] = dec * c_s[h, :, csl] + _dot_tn(wkb, v_ref[:, sl])
        n_s[h:h + 1, :] = dec * n_s[h:h + 1, :] + jnp.sum(wk, axis=0, keepdims=True)
        m_s[h:h + 1, :] = jnp.broadcast_to(m_new_all[h], (1, LANE))

    @pl.when(jnp.logical_and(is_tail, is_ctx))
    def _():
        cout_ref[...] = c_s[...]
        nout_ref[...] = n_s[...]
        mout_ref[...] = m_s[...]


def _mlstm_scan(proj, gates, b_gate, cache_c, cache_n, cache_m, j_layer, c_acc):
    H, L, S = ML_HEADS, ML_CHUNK, ML_SLOTS
    gd = jnp.stack([jnp.concatenate([gates[:, 0:H], gates[:, 2 * H:3 * H]], axis=1),
                    jnp.concatenate([gates[:, H:2 * H], gates[:, 3 * H:4 * H]], axis=1)], axis=0)
    gcol = gd.reshape(2, S.n, L, 2 * H)
    grow = gcol.transpose(0, 1, 3, 2)
    bd = jnp.stack([jnp.concatenate([b_gate[0:H], b_gate[2 * H:3 * H]]),
                    jnp.concatenate([b_gate[H:2 * H], b_gate[3 * H:4 * H]])], axis=0)
    bcol = bd.reshape(2, 1, 2 * H)
    brow = bd.reshape(2, 2 * H, 1)
    m0 = jnp.broadcast_to(cache_m[:, j_layer][..., None], (DEC_BATCH, 2, H, LANE))

    has_acc = c_acc is not None
    in_specs = [
        pl.BlockSpec((L, ML_QK), lambda d, g: (S.slot(d, g), 0)),
        pl.BlockSpec((L, ML_QK), lambda d, g: (S.slot(d, g), 1)),
        pl.BlockSpec((L, ML_INNER), lambda d, g: (S.slot(d, g), 1)),
        pl.BlockSpec((None, None, L, 2 * H), lambda d, g: (d, S.slot(d, g), 0, 0)),
        pl.BlockSpec((None, None, 2 * H, L), lambda d, g: (d, S.slot(d, g), 0, 0)),
        pl.BlockSpec((None, 1, 2 * H), lambda d, g: (d, 0, 0)),
        pl.BlockSpec((None, 2 * H, 1), lambda d, g: (d, 0, 0)),
        pl.BlockSpec((None, None, None, H, ML_DK, ML_DV), lambda d, g: (S.lat_batch(d, g), j_layer, d, 0, 0, 0)),
        pl.BlockSpec((None, None, None, H, ML_DK), lambda d, g: (S.lat_batch(d, g), j_layer, d, 0, 0)),
        pl.BlockSpec((None, None, H, LANE), lambda d, g: (S.lat_batch(d, g), d, 0, 0)),
    ]
    args = [proj, proj, proj, gcol, grow, bcol, brow, cache_c, cache_n, m0]
    aliases = {}
    if has_acc:
        in_specs.append(pl.BlockSpec(memory_space=pl.ANY))
        args.append(c_acc)
        aliases = {len(args) - 1: 1}

    return pl.pallas_call(
        functools.partial(_mlstm_kernel, has_acc=has_acc),
        grid=(2, S.n),
        in_specs=in_specs,
        out_specs=[
            pl.BlockSpec((None, L, ML_INNER), lambda d, g: (d, S.slot(d, g), 0)),
            pl.BlockSpec((None, None, None, H, ML_DK, ML_DV), lambda d, g: (S.ctx_seq(d, g), j_layer, d, 0, 0, 0)),
            pl.BlockSpec((None, None, H, ML_DK), lambda d, g: (S.ctx_seq(d, g), d, 0, 0)),
            pl.BlockSpec((None, None, H, LANE), lambda d, g: (S.ctx_seq(d, g), d, 0, 0)),
        ],
        out_shape=[
            jax.ShapeDtypeStruct((2, T_ALL, ML_INNER), BF16),
            jax.ShapeDtypeStruct((BATCH, N_ML, 2, H, ML_DK, ML_DV), F32),
            jax.ShapeDtypeStruct((BATCH, 2, H, ML_DK), F32),
            jax.ShapeDtypeStruct((BATCH, 2, H, LANE), F32),
        ],
        scratch_shapes=[
            pltpu.VMEM((H, ML_DK, ML_DV), F32),
            pltpu.VMEM((H, ML_DK), F32),
            pltpu.VMEM((H, LANE), F32),
        ],
        input_output_aliases=aliases,
        compiler_params=_cparams(2),
        name="mlstm_scan",
    )(*args)


def _mlstm_post_kernel(hf_ref, hb_ref, o_ref, z_ref, g_ref, y_ref):
    hs = hf_ref[...].astype(F32) + hb_ref[...].astype(F32)
    r = lax.rsqrt(jnp.mean(hs * hs, axis=-1, keepdims=True) + EPS)
    y = (hs * r * g_ref[...]) * _sigmoid(o_ref[...].astype(F32)) * _silu(z_ref[...].astype(F32))
    y_ref[...] = y.astype(BF16)


def _mlstm_post(hdir, proj, g_head):
    tm = 1024
    o_blk = (2 * ML_QK + ML_INNER) // ML_DV
    z_blk = (2 * ML_QK + 2 * ML_INNER) // ML_DV
    return pl.pallas_call(
        _mlstm_post_kernel,
        grid=(T_ALL // tm, ML_HEADS),
        in_specs=[
            pl.BlockSpec((None, tm, ML_DV), lambda i, h: (0, i, h)),
            pl.BlockSpec((None, tm, ML_DV), lambda i, h: (1, i, h)),
            pl.BlockSpec((tm, ML_DV), lambda i, h: (i, o_blk + h)),
            pl.BlockSpec((tm, ML_DV), lambda i, h: (i, z_blk + h)),
            pl.BlockSpec((1, ML_DV), lambda i, h: (0, h)),
        ],
        out_specs=pl.BlockSpec((tm, ML_DV), lambda i, h: (i, h)),
        out_shape=jax.ShapeDtypeStruct((T_ALL, ML_INNER), BF16),
        compiler_params=_cparams(2),
        name="mlstm_gate_norm",
    )(hdir, hdir, proj, proj, g_head.reshape(1, ML_INNER))


def _dwconv3_tile(x, w_ref, row_len, tm):
    r = lax.broadcasted_iota(jnp.int32, (tm, 1), 0) & (row_len - 1)
    prev = jnp.where(r == 0, 0.0, pltpu.roll(x, 1, axis=0))
    nxt = jnp.where(r == row_len - 1, 0.0, pltpu.roll(x, tm - 1, axis=0))
    return prev * w_ref[0:1, :] + x * w_ref[1:2, :] + nxt * w_ref[2:3, :]


def _sc_kernel(a_ref, wu_ref, wb_ref, wc_ref, wz_ref, cw_ref, y_ref, wbuf, *, tm):
    i = pl.program_id(1)

    @pl.when(i == 0)
    def _():
        for g, w_ref in enumerate((wu_ref, wb_ref, wc_ref, wz_ref)):
            wbuf[g] = w_ref[...].astype(BF16)

    a = a_ref[...]
    u, b, c, z = (_dot(a, wbuf[g]) for g in range(4))
    conv = _dwconv3_tile(c * u, cw_ref, _conv_row_len(i * tm), tm)
    y_ref[...] = (b * conv * _silu(z)).astype(BF16)


def _shortconv(a, w_in, layer, w_conv):
    tm, tn = 1024, 256
    k = a.shape[1]
    nb = SC_INNER // tn
    w_specs = [pl.BlockSpec((None, k, tn), functools.partial(lambda j, i, g: (layer, 0, g * nb + j), g=g))
               for g in range(4)]
    return pl.pallas_call(
        functools.partial(_sc_kernel, tm=tm),
        grid=(nb, T_ALL // tm),
        in_specs=[pl.BlockSpec((tm, k), lambda j, i: (i, 0)), *w_specs,
                  pl.BlockSpec((None, 3, tn), lambda j, i: (layer, 0, j))],
        out_specs=pl.BlockSpec((tm, tn), lambda j, i: (i, j)),
        out_shape=jax.ShapeDtypeStruct((T_ALL, SC_INNER), BF16),
        scratch_shapes=[pltpu.VMEM((4, k, tn), BF16)],
        compiler_params=_cparams(2),
        name="shortconv_mixer",
    )(a, w_in, w_in, w_in, w_in, w_conv)


def _gdn_qkv_kernel(a_ref, w_ref, cw_ref, o_ref, wb_ref, *, tm, tn):
    j = pl.program_id(0)
    i = pl.program_id(1)

    @pl.when(i == 0)
    def _():
        wb_ref[...] = w_ref[...].astype(BF16)

    y = _silu(_dwconv3_tile(_dot(a_ref[...], wb_ref[...]), cw_ref, _conv_row_len(i * tm), tm))
    q_blocks = GD_QK // tn
    is_qk = j < 2 * q_blocks
    q_scale = jnp.where(j < q_blocks, GD_DK ** -0.5, 1.0)
    for hh in range(tn // GD_DK):
        yh = y[:, hh * GD_DK:(hh + 1) * GD_DK]
        r = lax.rsqrt(jnp.sum(yh * yh, axis=-1, keepdims=True) + EPS) * q_scale
        o_ref[:, hh * GD_DK:(hh + 1) * GD_DK] = (yh * jnp.where(is_qk, r, 1.0)).astype(o_ref.dtype)


def _gdn_qkv(a, w_in, layer, w_conv):
    tm, tn = 1024, 1024
    k = a.shape[1]
    return pl.pallas_call(
        functools.partial(_gdn_qkv_kernel, tm=tm, tn=tn),
        grid=(GD_CONV_CH // tn, T_ALL // tm),
        in_specs=[
            pl.BlockSpec((tm, k), lambda j, i: (i, 0)),
            pl.BlockSpec((None, k, tn), lambda j, i: (layer, 0, j)),
            pl.BlockSpec((None, 3, tn), lambda j, i: (layer, 0, j)),
        ],
        out_specs=pl.BlockSpec((tm, tn), lambda j, i: (i, j)),
        out_shape=jax.ShapeDtypeStruct((T_ALL, GD_CONV_CH), BF16),
        scratch_shapes=[pltpu.VMEM((k, tn), BF16)],
        compiler_params=_cparams(2),
        name="gdn_qkv_projection",
    )(a, w_in, w_conv)


def _gdn_kernel(q_ref, k_ref, v_ref, gc_ref, gr_ref, pc_ref, pr_ref, s0_ref,
                o_ref, sout_ref, s_s):
    L = GD_CHUNK
    d = pl.program_id(0)
    j = GD_SLOTS.slot(d, pl.program_id(2))
    is_ctx, is_head, is_tail = GD_SLOTS.flags(d, j)

    @pl.when(jnp.logical_and(is_head, is_ctx))
    def _():
        s_s[...] = jnp.zeros_like(s_s)

    @pl.when(jnp.logical_and(is_head, jnp.logical_not(is_ctx)))
    def _():
        s_s[...] = s0_ref[...]

    incl, strict = GD_SLOTS.mask(d)
    tri = incl.astype(F32)
    nh = GD_GV
    gc = gc_ref[...]
    gr = gr_ref[...]
    g_c = -jnp.exp(pc_ref[0:1, :]) * _softplus(gc[:, 0:nh] + pc_ref[1:2, :])
    G_c = _dot(tri, g_c, precision=HIGHEST)
    beta_c = _sigmoid(gc[:, nh:2 * nh])
    g_r = -jnp.exp(pr_ref[:, 0:1]) * _softplus(gr[0:nh, :] + pr_ref[:, 1:2])
    G_r = _dot_nt(g_r, tri, precision=HIGHEST)
    Gl_all = jnp.where(d == 0, G_c[L - 1:L, :], G_c[0:1, :])

    rep = GD_V_HEADS // GD_QK_HEADS
    heads = range(nh)
    k32, qb, kk, qk = [], [], [], []
    for hq in range(GD_GQ):
        kb = k_ref[:, hq * GD_DK:(hq + 1) * GD_DK]
        k32.append(kb.astype(F32))
        qb.append(q_ref[:, hq * GD_DK:(hq + 1) * GD_DK])
        both = _dot_nt(jnp.concatenate([kb, qb[hq]], axis=0), kb)
        kk.append(both[0:L])
        qk.append(both[L:2 * L])

    decay, neg_a, rhs, eG = [], [], [], []
    for hv in heads:
        Gc = G_c[:, hv:hv + 1]
        beta = beta_c[:, hv:hv + 1]
        decay.append(jnp.where(incl, jnp.exp(jnp.where(incl, Gc - G_r[hv:hv + 1, :], 0.0)), 0.0))
        neg_a.append(jnp.where(strict, -(beta * kk[hv // rep] * decay[hv]), 0.0))
        eG.append(jnp.exp(Gc))
        v32 = v_ref[:, hv * GD_DV:(hv + 1) * GD_DV].astype(F32)
        rhs.append(jnp.concatenate([(beta * eG[hv]) * k32[hv // rep], beta * v32], axis=-1))

    n_hi = [a.astype(BF16) for a in neg_a]
    m_inv = list(neg_a)
    p32 = [_dot(p, p) for p in n_hi]
    for k in range(1, 6):
        pw = [x.astype(BF16) for x in p32]
        if k < 5:
            both = [_dot(jnp.concatenate([p, m.astype(BF16)], axis=0), p) for p, m in zip(pw, m_inv)]
            m_inv = [m + x + b[L:2 * L] for m, x, b in zip(m_inv, p32, both)]
            p32 = [b[0:L] for b in both]
        else:
            m_inv = [m + x + _dot(m.astype(BF16), p) for m, x, p in zip(m_inv, p32, pw)]
    mb = [m.astype(BF16) for m in m_inv]
    x1 = [r + _dot(m, r.astype(BF16)) for m, r in zip(mb, rhs)]
    res = []
    for r, x, a, ah in zip(rhs, x1, neg_a, n_hi):
        al = (a - ah.astype(F32)).astype(BF16)
        xh = x.astype(BF16)
        xl = (x - xh.astype(F32)).astype(BF16)
        hi = _dot(jnp.concatenate([ah, al], axis=0), xh)
        res.append((r - x) + (hi[0:L] + hi[L:2 * L] + _dot(ah, xl)))
    sol = [x + r + _dot(m, r.astype(BF16)) for x, r, m in zip(x1, res, mb)]

    S = [s_s[hv] for hv in heads]
    Sb = [x.astype(BF16) for x in S]
    wq = [_dot(jnp.concatenate([sol[hv][:, 0:GD_DK].astype(BF16), qb[hv // rep]], axis=0), Sb[hv]) for hv in heads]
    qs = [x[L:2 * L] for x in wq]
    ub = [(sol[hv][:, GD_DK:GD_DK + GD_DV] - wq[hv][0:L]).astype(BF16) for hv in heads]
    pu = [_dot((qk[hv // rep] * decay[hv]).astype(BF16), ub[hv]) for hv in heads]
    kdec = [(jnp.exp(Gl_all[:, hv:hv + 1] - G_c[:, hv:hv + 1]) * k32[hv // rep]).astype(BF16) for hv in heads]
    ku = [_dot_tn(kdec[hv], ub[hv]) for hv in heads]
    for hv in heads:
        o_ref[:, hv * GD_DV:(hv + 1) * GD_DV] = (eG[hv] * qs[hv] + pu[hv]).astype(o_ref.dtype)
        s_s[hv] = jnp.exp(Gl_all[:, hv:hv + 1]) * S[hv] + ku[hv]

    @pl.when(jnp.logical_and(is_tail, is_ctx))
    def _():
        sout_ref[...] = s_s[...]


def _gdn_scan(qkv, ab, a_log, dt_bias, cache_s, j_layer):
    HV, nh, L, S = GD_V_HEADS, GD_GV, GD_CHUNK, GD_SLOTS
    a_d = ab[:, 0:2 * HV].reshape(T_ALL, 2, GD_GROUPS, nh)
    b_d = ab[:, 2 * HV:4 * HV].reshape(T_ALL, 2, GD_GROUPS, nh)
    gcol = jnp.concatenate([a_d, b_d], axis=-1).reshape(S.n, L, 2, GD_GROUPS, 2 * nh)
    gcol = gcol.transpose(2, 3, 0, 1, 4)
    grow = gcol.transpose(0, 1, 2, 4, 3)
    par = jnp.stack([a_log.astype(F32), dt_bias.astype(F32)], axis=1)
    pcol = par.reshape(2, 2, GD_GROUPS, nh).transpose(0, 2, 1, 3)
    prow = pcol.transpose(0, 1, 3, 2)

    qb = GD_GQ * GD_DK
    vb = nh * GD_DV
    return pl.pallas_call(
        _gdn_kernel,
        grid=(2, GD_GROUPS, S.n),
        in_specs=[
            pl.BlockSpec((L, qb), lambda d, p, g: (S.slot(d, g), p)),
            pl.BlockSpec((L, qb), lambda d, p, g: (S.slot(d, g), GD_GROUPS + p)),
            pl.BlockSpec((L, vb), lambda d, p, g: (S.slot(d, g), GD_GROUPS + p)),
            pl.BlockSpec((None, None, None, L, 2 * nh), lambda d, p, g: (d, p, S.slot(d, g), 0, 0)),
            pl.BlockSpec((None, None, None, 2 * nh, L), lambda d, p, g: (d, p, S.slot(d, g), 0, 0)),
            pl.BlockSpec((None, None, 2, nh), lambda d, p, g: (d, p, 0, 0)),
            pl.BlockSpec((None, None, nh, 2), lambda d, p, g: (d, p, 0, 0)),
            pl.BlockSpec((None, None, None, nh, GD_DK, GD_DV),
                         lambda d, p, g: (S.lat_batch(d, g), j_layer, d, p, 0, 0)),
        ],
        out_specs=[
            pl.BlockSpec((None, L, vb), lambda d, p, g: (d, S.slot(d, g), p)),
            pl.BlockSpec((None, None, nh, GD_DK, GD_DV), lambda d, p, g: (S.ctx_seq(d, g), d, p, 0, 0)),
        ],
        out_shape=[
            jax.ShapeDtypeStruct((2, T_ALL, GD_INNER), BF16),
            jax.ShapeDtypeStruct((BATCH, 2, HV, GD_DK, GD_DV), F32),
        ],
        scratch_shapes=[pltpu.VMEM((nh, GD_DK, GD_DV), F32)],
        compiler_params=_cparams(3),
        name="gdn_scan",
    )(qkv, qkv, qkv, gcol, grow, pcol, prow, cache_s)


def _gdn_post_kernel(of_ref, ob_ref, z_ref, g_ref, y_ref, *, tn):
    g = g_ref[...]
    for hh in range(tn // GD_DV):
        sl = slice(hh * GD_DV, (hh + 1) * GD_DV)
        o = of_ref[:, sl].astype(F32) + ob_ref[:, sl].astype(F32)
        r = lax.rsqrt(jnp.mean(o * o, axis=-1, keepdims=True) + EPS)
        y_ref[:, sl] = ((o * r * g) * _silu(z_ref[:, sl].astype(F32))).astype(BF16)


def _gdn_post(odir, z, g_norm):
    tm, tn = 1024, 512
    return pl.pallas_call(
        functools.partial(_gdn_post_kernel, tn=tn),
        grid=(T_ALL // tm, GD_INNER // tn),
        in_specs=[
            pl.BlockSpec((None, tm, tn), lambda i, j: (0, i, j)),
            pl.BlockSpec((None, tm, tn), lambda i, j: (1, i, j)),
            pl.BlockSpec((tm, tn), lambda i, j: (i, j)),
            pl.BlockSpec((1, GD_DV), lambda i, j: (0, 0)),
        ],
        out_specs=pl.BlockSpec((tm, tn), lambda i, j: (i, j)),
        out_shape=jax.ShapeDtypeStruct((T_ALL, GD_INNER), BF16),
        compiler_params=_cparams(2),
        name="gdn_gate_norm",
    )(odir, odir, z, g_norm.reshape(1, GD_DV))


def kernel(x_prompt, x_sample, c, cache_ml_C, cache_ml_n, cache_ml_m, cache_gd_S, c_ctx, w_ada, b_ada, g_norm,
           w_ml_in, b_ml_gate, g_ml_head, w_ml_out, w_sc_in, w_sc_conv, w_sc_out, w_gd_in, w_gd_conv, gd_A_log,
           gd_dt_bias, g_gd_norm, w_gd_out, g_final):
    xs = (x_prompt.reshape(T_CTX, D_MODEL), x_sample.reshape(T_LAT, D_MODEL))
    cond = jnp.concatenate([c_ctx[None, :], c, jnp.zeros((N_COND - 1 - DEC_BATCH, D_MODEL), F32)], axis=0)
    mod = _modulation(cond, w_ada, b_ada)
    w_ml_nk = jnp.swapaxes(w_ml_in, 1, 2)

    state_ml_c = None
    ml_n, ml_m, gd_s = [], [], []
    for l in range(DEPTH):
        shift = mod[l, :, 0:D_MODEL]
        scale = mod[l, :, D_MODEL:2 * D_MODEL]
        gate = mod[l, :, 2 * D_MODEL:3 * D_MODEL]
        h = _norm_mod(xs, g_norm[l], shift, scale)
        j = l // 3
        kind = l % 3
        if kind == 0:
            proj = _project(h, w_ml_nk, j, 0, ML_MAIN, 1024, BF16, w_is_nk=True)
            gates = _project(h, w_ml_nk, j, ML_MAIN, 4 * ML_HEADS, 4 * ML_HEADS, F32, w_is_nk=True)
            hdir, state_ml_c, n_fin, m_fin = _mlstm_scan(proj, gates, b_ml_gate[j], cache_ml_C, cache_ml_n,
                                                         cache_ml_m, j, state_ml_c)
            ml_n.append(n_fin)
            ml_m.append(m_fin[..., 0])
            y = _mlstm_post(hdir, proj, g_ml_head[j])
            xs = (_project_residual(y, w_ml_out, j, xs, gate),)
        elif kind == 1:
            y = _shortconv(h, w_sc_in, j, w_sc_conv)
            xs = (_project_residual(y, w_sc_out, j, xs, gate),)
        else:
            qkv = _gdn_qkv(h, w_gd_in, j, w_gd_conv)
            z = _project(h, w_gd_in, j, GD_CONV_CH, GD_INNER, 1024, BF16)
            ab = _project(h, w_gd_in, j, GD_MAIN, 4 * GD_V_HEADS, 4 * GD_V_HEADS, F32)
            odir, s_fin = _gdn_scan(qkv, ab, gd_A_log[j], gd_dt_bias[j], cache_gd_S, j)
            gd_s.append(s_fin)
            y = _gdn_post(odir, z, g_gd_norm[j])
            xs = (_project_residual(y, w_gd_out, j, xs, gate),)

    y_ctx, y_lat = _final_norm(xs[0], g_final)
    y_prompt = y_ctx.reshape(BATCH, SEQ, D_MODEL)
    y_sample = y_lat.reshape(DEC_BATCH, DEC_SEQ, D_MODEL)
    state_ml_n = jnp.stack(ml_n, axis=1)
    state_ml_m = jnp.stack(ml_m, axis=1)
    state_gd_s = jnp.stack(gd_s, axis=1)
    return (y_prompt, y_sample, state_ml_c, state_ml_n, state_ml_m, state_gd_s)
```

```python
import functools

import jax
import jax.numpy as jnp
from jax import lax
from jax.experimental import pallas as pl
from jax.experimental.pallas import tpu as pltpu

F32 = jnp.float32
BF16 = jnp.bfloat16
HIGHEST = lax.Precision.HIGHEST

D_MODEL = 2048
BATCH = 16
SEQ = 256
DEPTH = 4
DEC_BATCH = 2
DEC_SEQ = 2048
GRID_W = 64
EPS = 1e-6

T_CTX = BATCH * SEQ
T_LAT = DEC_BATCH * DEC_SEQ
T_ALL = T_CTX + T_LAT
N_COND = 8

ML_HEADS = 8
ML_DK = 256
ML_DV = 512
ML_QK = ML_HEADS * ML_DK
ML_INNER = ML_HEADS * ML_DV
ML_MAIN = 2 * ML_QK + 3 * ML_INNER
N_ML = 2
ML_CHUNK = 256
ML_CBLK = 256

SC_INNER = 2 * D_MODEL

GD_DK = 128
GD_DV = 128
GD_QK_HEADS = 16
GD_V_HEADS = 32
GD_QK = GD_QK_HEADS * GD_DK
GD_INNER = GD_V_HEADS * GD_DV
GD_CONV_CH = 2 * GD_QK + GD_INNER
GD_MAIN = GD_CONV_CH + GD_INNER
GD_GROUPS = 1
GD_GV = GD_V_HEADS // GD_GROUPS
GD_GQ = GD_QK_HEADS // GD_GROUPS
GD_CHUNK = 64

LANE = 128
BF16_ROWS = 16
VMEM_LIMIT = 56 * 1024 * 1024


def _cparams(n_axes):
    return pltpu.CompilerParams(dimension_semantics=("arbitrary",) * n_axes, vmem_limit_bytes=VMEM_LIMIT)


def _sigmoid(x):
    return 0.5 * jnp.tanh(0.5 * x) + 0.5


def _silu(x):
    h = 0.5 * x
    return h + h * jnp.tanh(h)


def _softplus(x):
    return jnp.maximum(x, 0.0) + jnp.log1p(jnp.exp(-jnp.abs(x)))


def _log_sigmoid(x):
    return -_softplus(-x)


def _cond_group(row0):
    return jnp.where(row0 < T_CTX, 0, 1 + (row0 - T_CTX) // DEC_SEQ)


def _conv_row_len(row0):
    return jnp.where(row0 < T_CTX, SEQ, GRID_W)


def _dot_nt(a, b, **kw):
    return lax.dot_general(a, b, (((1,), (1,)), ((), ())), preferred_element_type=F32, **kw)


def _dot_tn(a, b, **kw):
    return lax.dot_general(a, b, (((0,), (0,)), ((), ())), preferred_element_type=F32, **kw)


def _dot(a, b, **kw):
    return jnp.dot(a, b, preferred_element_type=F32, **kw)


def _mod_kernel(s_ref, w_ref, b_ref, o_ref):
    a = _silu(s_ref[...]).astype(BF16)
    o_ref[...] = _dot(a, w_ref[...].astype(BF16)) + b_ref[...]


def _modulation(cond, w_ada, b_ada):
    tn = 1024
    n = 3 * D_MODEL
    return pl.pallas_call(
        _mod_kernel,
        grid=(DEPTH, n // tn),
        in_specs=[
            pl.BlockSpec((N_COND, D_MODEL), lambda l, j: (0, 0)),
            pl.BlockSpec((None, D_MODEL, tn), lambda l, j: (l, 0, j)),
            pl.BlockSpec((None, 1, tn), lambda l, j: (l, 0, j)),
        ],
        out_specs=pl.BlockSpec((None, N_COND, tn), lambda l, j: (l, 0, j)),
        out_shape=jax.ShapeDtypeStruct((DEPTH, N_COND, n), F32),
        compiler_params=_cparams(2),
        name="adaln_modulation",
    )(cond, w_ada, b_ada.reshape(DEPTH, 1, n))


def _split_specs(block, index_map, tm):
    n_ctx = T_CTX // tm

    def ctx_map(*idx):
        row, *rest = index_map(*idx)
        return (jnp.minimum(row, n_ctx - 1), *rest)

    def lat_map(*idx):
        row, *rest = index_map(*idx)
        return (jnp.maximum(row - n_ctx, 0), *rest)

    return [pl.BlockSpec(block, ctx_map), pl.BlockSpec(block, lat_map)]


def _token_tile(refs, row0):
    if len(refs) == 1:
        return refs[0][...]
    return jnp.where(row0 < T_CTX, refs[0][...], refs[1][...])


def _norm_mod_kernel(*refs, tm):
    *x_refs, g_ref, shift_ref, scale_ref, o_ref = refs
    row0 = pl.program_id(0) * tm
    grp = _cond_group(row0)
    x = _token_tile(x_refs, row0)
    r = lax.rsqrt(jnp.mean(x * x, axis=-1, keepdims=True) + EPS)
    xn = x * r * g_ref[...]
    sh = shift_ref[pl.ds(grp, 1), :]
    sc = scale_ref[pl.ds(grp, 1), :]
    o_ref[...] = (xn * (1.0 + sc) + sh).astype(BF16)


def _norm_mod(xs, g, shift, scale):
    tm = 512
    x_map = lambda i: (i, 0)
    x_specs = [pl.BlockSpec((tm, D_MODEL), x_map)] if len(xs) == 1 else _split_specs((tm, D_MODEL), x_map, tm)
    return pl.pallas_call(
        functools.partial(_norm_mod_kernel, tm=tm),
        grid=(T_ALL // tm,),
        in_specs=[
            *x_specs,
            pl.BlockSpec((1, D_MODEL), lambda i: (0, 0)),
            pl.BlockSpec((N_COND, D_MODEL), lambda i: (0, 0)),
            pl.BlockSpec((N_COND, D_MODEL), lambda i: (0, 0)),
        ],
        out_specs=pl.BlockSpec((tm, D_MODEL), lambda i: (i, 0)),
        out_shape=jax.ShapeDtypeStruct((T_ALL, D_MODEL), BF16),
        compiler_params=_cparams(1),
        name="norm_modulate",
    )(*xs, g.reshape(1, D_MODEL), shift, scale)


def _final_norm_kernel(x_ref, g_ref, ctx_ref, lat_ref, *, ctx_tiles):
    x = x_ref[...]
    r = lax.rsqrt(jnp.mean(x * x, axis=-1, keepdims=True) + EPS)
    y = x * r * g_ref[...]
    i = pl.program_id(0)

    @pl.when(i < ctx_tiles)
    def _():
        ctx_ref[...] = y

    @pl.when(i >= ctx_tiles)
    def _():
        lat_ref[...] = y


def _final_norm(x, g):
    tm = 512
    ctx_tiles = T_CTX // tm
    return pl.pallas_call(
        functools.partial(_final_norm_kernel, ctx_tiles=ctx_tiles),
        grid=(T_ALL // tm,),
        in_specs=[
            pl.BlockSpec((tm, D_MODEL), lambda i: (i, 0)),
            pl.BlockSpec((1, D_MODEL), lambda i: (0, 0)),
        ],
        out_specs=[
            pl.BlockSpec((tm, D_MODEL), lambda i: (jnp.minimum(i, ctx_tiles - 1), 0)),
            pl.BlockSpec((tm, D_MODEL), lambda i: (jnp.maximum(i - ctx_tiles, 0), 0)),
        ],
        out_shape=[
            jax.ShapeDtypeStruct((T_CTX, D_MODEL), F32),
            jax.ShapeDtypeStruct((T_LAT, D_MODEL), F32),
        ],
        compiler_params=_cparams(1),
        name="final_norm",
    )(x, g.reshape(1, D_MODEL))


def _proj_kernel(a_ref, w_ref, o_ref, wb_ref, *, w_is_nk):
    @pl.when(pl.program_id(1) == 0)
    def _():
        wb_ref[...] = w_ref[...].astype(BF16)

    dot = _dot_nt if w_is_nk else _dot
    o_ref[...] = dot(a_ref[...], wb_ref[...]).astype(o_ref.dtype)


def _project(a, w, layer, col0, n_cols, tn, out_dtype, w_is_nk=False, tm=1024):
    k = a.shape[1]
    j0 = col0 // tn
    if w_is_nk:
        w_spec = pl.BlockSpec((None, tn, k), lambda j, i: (layer, j0 + j, 0))
        w_tile = (tn, k)
    else:
        w_spec = pl.BlockSpec((None, k, tn), lambda j, i: (layer, 0, j0 + j))
        w_tile = (k, tn)
    return pl.pallas_call(
        functools.partial(_proj_kernel, w_is_nk=w_is_nk),
        grid=(n_cols // tn, T_ALL // tm),
        in_specs=[pl.BlockSpec((tm, k), lambda j, i: (i, 0)), w_spec],
        out_specs=pl.BlockSpec((tm, tn), lambda j, i: (i, j)),
        out_shape=jax.ShapeDtypeStruct((T_ALL, n_cols), out_dtype),
        scratch_shapes=[pltpu.VMEM(w_tile, BF16)],
        compiler_params=_cparams(2),
        name="in_projection",
    )(a, w)


def _proj_res_kernel(a_ref, w_ref, *refs, tm):
    *x_refs, gate_ref, o_ref, wb_ref = refs
    i = pl.program_id(1)

    @pl.when(i == 0)
    def _():
        wb_ref[...] = w_ref[...].astype(BF16)

    gate = gate_ref[pl.ds(_cond_group(i * tm), 1), :]
    o_ref[...] = _token_tile(x_refs, i * tm) + gate * _dot(a_ref[...], wb_ref[...])


def _project_residual(a, w, layer, xs, gate):
    k = a.shape[1]
    tm, tn = 1024, 512
    x_map = lambda j, i: (i, j)
    x_specs = [pl.BlockSpec((tm, tn), x_map)] if len(xs) == 1 else _split_specs((tm, tn), x_map, tm)
    return pl.pallas_call(
        functools.partial(_proj_res_kernel, tm=tm),
        grid=(D_MODEL // tn, T_ALL // tm),
        in_specs=[
            pl.BlockSpec((tm, k), lambda j, i: (i, 0)),
            pl.BlockSpec((None, k, tn), lambda j, i: (layer, 0, j)),
            *x_specs,
            pl.BlockSpec((N_COND, tn), lambda j, i: (0, j)),
        ],
        out_specs=pl.BlockSpec((tm, tn), lambda j, i: (i, j)),
        out_shape=jax.ShapeDtypeStruct((T_ALL, D_MODEL), F32),
        scratch_shapes=[pltpu.VMEM((k, tn), BF16)],
        compiler_params=_cparams(2),
        name="out_projection_residual",
    )(a, w, *xs, gate)


class _Slots:
    def __init__(self, chunk):
        self.chunk = chunk
        self.n = T_ALL // chunk
        self.ctx = T_CTX // chunk
        self.ctx_chunks = SEQ // chunk
        self.lat_chunks = DEC_SEQ // chunk

    def slot(self, d, g):
        return g + d * (self.n - 1 - 2 * g)

    def seq(self, j):
        return jnp.where(j < self.ctx, j // self.ctx_chunks, BATCH + (j - self.ctx) // self.lat_chunks)

    def flags(self, d, j):
        is_ctx = j < self.ctx
        pos = jnp.where(is_ctx, j % self.ctx_chunks, (j - self.ctx) % self.lat_chunks)
        n_chunks = jnp.where(is_ctx, self.ctx_chunks, self.lat_chunks)
        head_pos = jnp.where(d == 0, 0, n_chunks - 1)
        tail_pos = jnp.where(d == 0, n_chunks - 1, 0)
        return is_ctx, pos == head_pos, pos == tail_pos

    def mask(self, d):
        t_i = lax.broadcasted_iota(jnp.int32, (self.chunk, self.chunk), 0)
        s_i = lax.broadcasted_iota(jnp.int32, (self.chunk, self.chunk), 1)
        diff = (t_i - s_i) * (1 - 2 * d)
        return diff >= 0, diff > 0

    def lat_batch(self, d, g):
        return jnp.clip(self.seq(self.slot(d, g)) - BATCH, 0, DEC_BATCH - 1)

    def ctx_seq(self, d, g):
        return jnp.minimum(self.seq(self.slot(d, g)), BATCH - 1)


ML_SLOTS = _Slots(ML_CHUNK)
GD_SLOTS = _Slots(GD_CHUNK)


def _mlstm_kernel(q_ref, k_ref, v_ref, gc_ref, gr_ref, bc_ref, br_ref, c0_ref, n0_ref, m0_ref, *rest,
                  has_acc):
    if has_acc:
        rest = rest[1:]
    h_ref, cout_ref, nout_ref, mout_ref, c_s, n_s, m_s = rest
    L = ML_CHUNK
    d = pl.program_id(0)
    j = ML_SLOTS.slot(d, pl.program_id(1))
    is_ctx, is_head, is_tail = ML_SLOTS.flags(d, j)

    @pl.when(jnp.logical_and(is_head, is_ctx))
    def _():
        c_s[...] = jnp.zeros_like(c_s)
        n_s[...] = jnp.zeros_like(n_s)
        m_s[...] = jnp.zeros_like(m_s)

    @pl.when(jnp.logical_and(is_head, jnp.logical_not(is_ctx)))
    def _():
        c_s[...] = c0_ref[...]
        n_s[...] = n0_ref[...]
        m_s[...] = m0_ref[...]

    mask, _ = ML_SLOTS.mask(d)
    tri = mask.astype(F32)
    gc = gc_ref[...] + bc_ref[...]
    gr = gr_ref[...] + br_ref[...]
    i_c = gc[:, 0:ML_HEADS]
    b_c = _dot(tri, _log_sigmoid(gc[:, ML_HEADS:2 * ML_HEADS]), precision=HIGHEST)
    i_r = gr[0:ML_HEADS, :]
    b_r = _dot_nt(_log_sigmoid(gr[ML_HEADS:2 * ML_HEADS, :]), tri, precision=HIGHEST)
    bl_all = jnp.where(d == 0, b_c[L - 1:L, :], b_c[0:1, :])

    a_all, m_new_all, dec_all = [], [], []
    qbs = [q_ref[:, h * ML_DK:(h + 1) * ML_DK] * (ML_DK ** -0.5) for h in range(ML_HEADS)]
    z_all, mt_all, sc_all = [], [], []
    for h in range(ML_HEADS):
        bcol = b_c[:, h:h + 1]
        m = m_s[h:h + 1, 0:1]
        z = jnp.where(mask, i_r[h:h + 1, :] - b_r[h:h + 1, :], -jnp.inf)
        inter = bcol + m
        mt = jnp.maximum(inter, bcol + jnp.max(z, axis=-1, keepdims=True))
        z_all.append(bcol - mt)
        mt_all.append(mt)
        sc_all.append(jnp.exp(inter - mt))
        bl = bl_all[:, h:h + 1]
        a = bl - bcol + i_c[:, h:h + 1]
        m_new = jnp.maximum(bl + m, jnp.max(a, axis=0, keepdims=True))
        a_all.append(a)
        m_new_all.append(m_new)
        dec_all.append(jnp.exp(bl + m - m_new))
    qkn = []
    for h in range(ML_HEADS):
        nb = jnp.broadcast_to(n_s[h:h + 1, :], (BF16_ROWS, ML_DK)).astype(BF16)
        qkn.append(_dot_nt(qbs[h], jnp.concatenate([k_ref[:, h * ML_DK:(h + 1) * ML_DK], nb], axis=0)))
    lhs, rinv = [], []
    for h in range(ML_HEADS):
        z = jnp.where(mask, i_r[h:h + 1, :] - b_r[h:h + 1, :], -jnp.inf)
        s = qkn[h][:, 0:L] * jnp.exp(z + z_all[h])
        den = sc_all[h] * qkn[h][:, L:L + 1] + jnp.sum(s, axis=-1, keepdims=True)
        rinv.append(1.0 / jnp.maximum(jnp.abs(den), jnp.exp(-mt_all[h])))
        lhs.append(jnp.concatenate([(qbs[h].astype(F32) * sc_all[h]).astype(BF16), s.astype(BF16)], axis=1))
    for h in range(ML_HEADS):
        w = jnp.concatenate([c_s[h].astype(BF16), v_ref[:, h * ML_DV:(h + 1) * ML_DV]], axis=0)
        h_ref[:, h * ML_DV:(h + 1) * ML_DV] = (_dot(lhs[h], w) * rinv[h]).astype(h_ref.dtype)

    for h in range(ML_HEADS):
        wk = jnp.exp(a_all[h] - m_new_all[h]) * k_ref[:, h * ML_DK:(h + 1) * ML_DK].astype(F32)
        wkb = wk.astype(BF16)
        dec = dec_all[h]
        for cb in range(ML_DV // ML_CBLK):
            sl = slice(h * ML_DV + cb * ML_CBLK, h * ML_DV + (cb + 1) * ML_CBLK)
            csl = slice(cb * ML_CBLK, (cb + 1) * ML_CBLK)
            c_s[h, :, csl] = dec * c_s[h, :, csl] + _dot_tn(wkb, v_ref[:, sl])
        n_s[h:h + 1, :] = dec * n_s[h:h + 1, :] + jnp.sum(wk, axis=0, keepdims=True)
        m_s[h:h + 1, :] = jnp.broadcast_to(m_new_all[h], (1, LANE))

    @pl.when(jnp.logical_and(is_tail, is_ctx))
    def _():
        cout_ref[...] = c_s[...]
        nout_ref[...] = n_s[...]
        mout_ref[...] = m_s[...]


def _mlstm_scan(proj, gates, b_gate, cache_c, cache_n, cache_m, j_layer, c_acc):
    H, L, S = ML_HEADS, ML_CHUNK, ML_SLOTS
    gd = jnp.stack([jnp.concatenate([gates[:, 0:H], gates[:, 2 * H:3 * H]], axis=1),
                    jnp.concatenate([gates[:, H:2 * H], gates[:, 3 * H:4 * H]], axis=1)], axis=0)
    gcol = gd.reshape(2, S.n, L, 2 * H)
    grow = gcol.transpose(0, 1, 3, 2)
    bd = jnp.stack([jnp.concatenate([b_gate[0:H], b_gate[2 * H:3 * H]]),
                    jnp.concatenate([b_gate[H:2 * H], b_gate[3 * H:4 * H]])], axis=0)
    bcol = bd.reshape(2, 1, 2 * H)
    brow = bd.reshape(2, 2 * H, 1)
    m0 = jnp.broadcast_to(cache_m[:, j_layer][..., None], (DEC_BATCH, 2, H, LANE))

    has_acc = c_acc is not None
    in_specs = [
        pl.BlockSpec((L, ML_QK), lambda d, g: (S.slot(d, g), 0)),
        pl.BlockSpec((L, ML_QK), lambda d, g: (S.slot(d, g), 1)),
        pl.BlockSpec((L, ML_INNER), lambda d, g: (S.slot(d, g), 1)),
        pl.BlockSpec((None, None, L, 2 * H), lambda d, g: (d, S.slot(d, g), 0, 0)),
        pl.BlockSpec((None, None, 2 * H, L), lambda d, g: (d, S.slot(d, g), 0, 0)),
        pl.BlockSpec((None, 1, 2 * H), lambda d, g: (d, 0, 0)),
        pl.BlockSpec((None, 2 * H, 1), lambda d, g: (d, 0, 0)),
        pl.BlockSpec((None, None, None, H, ML_DK, ML_DV), lambda d, g: (S.lat_batch(d, g), j_layer, d, 0, 0, 0)),
        pl.BlockSpec((None, None, None, H, ML_DK), lambda d, g: (S.lat_batch(d, g), j_layer, d, 0, 0)),
        pl.BlockSpec((None, None, H, LANE), lambda d, g: (S.lat_batch(d, g), d, 0, 0)),
    ]
    args = [proj, proj, proj, gcol, grow, bcol, brow, cache_c, cache_n, m0]
    aliases = {}
    if has_acc:
        in_specs.append(pl.BlockSpec(memory_space=pl.ANY))
        args.append(c_acc)
        aliases = {len(args) - 1: 1}

    return pl.pallas_call(
        functools.partial(_mlstm_kernel, has_acc=has_acc),
        grid=(2, S.n),
        in_specs=in_specs,
        out_specs=[
            pl.BlockSpec((None, L, ML_INNER), lambda d, g: (d, S.slot(d, g), 0)),
            pl.BlockSpec((None, None, None, H, ML_DK, ML_DV), lambda d, g: (S.ctx_seq(d, g), j_layer, d, 0, 0, 0)),
            pl.BlockSpec((None, None, H, ML_DK), lambda d, g: (S.ctx_seq(d, g), d, 0, 0)),
            pl.BlockSpec((None, None, H, LANE), lambda d, g: (S.ctx_seq(d, g), d, 0, 0)),
        ],
        out_shape=[
            jax.ShapeDtypeStruct((2, T_ALL, ML_INNER), BF16),
            jax.ShapeDtypeStruct((BATCH, N_ML, 2, H, ML_DK, ML_DV), F32),
            jax.ShapeDtypeStruct((BATCH, 2, H, ML_DK), F32),
            jax.ShapeDtypeStruct((BATCH, 2, H, LANE), F32),
        ],
        scratch_shapes=[
            pltpu.VMEM((H, ML_DK, ML_DV), F32),
            pltpu.VMEM((H, ML_DK), F32),
            pltpu.VMEM((H, LANE), F32),
        ],
        input_output_aliases=aliases,
        compiler_params=_cparams(2),
        name="mlstm_scan",
    )(*args)


def _mlstm_post_kernel(hf_ref, hb_ref, o_ref, z_ref, g_ref, y_ref):
    hs = hf_ref[...].astype(F32) + hb_ref[...].astype(F32)
    r = lax.rsqrt(jnp.mean(hs * hs, axis=-1, keepdims=True) + EPS)
    y = (hs * r * g_ref[...]) * _sigmoid(o_ref[...].astype(F32)) * _silu(z_ref[...].astype(F32))
    y_ref[...] = y.astype(BF16)


def _mlstm_post(hdir, proj, g_head):
    tm = 1024
    o_blk = (2 * ML_QK + ML_INNER) // ML_DV
    z_blk = (2 * ML_QK + 2 * ML_INNER) // ML_DV
    return pl.pallas_call(
        _mlstm_post_kernel,
        grid=(T_ALL // tm, ML_HEADS),
        in_specs=[
            pl.BlockSpec((None, tm, ML_DV), lambda i, h: (0, i, h)),
            pl.BlockSpec((None, tm, ML_DV), lambda i, h: (1, i, h)),
            pl.BlockSpec((tm, ML_DV), lambda i, h: (i, o_blk + h)),
            pl.BlockSpec((tm, ML_DV), lambda i, h: (i, z_blk + h)),
            pl.BlockSpec((1, ML_DV), lambda i, h: (0, h)),
        ],
        out_specs=pl.BlockSpec((tm, ML_DV), lambda i, h: (i, h)),
        out_shape=jax.ShapeDtypeStruct((T_ALL, ML_INNER), BF16),
        compiler_params=_cparams(2),
        name="mlstm_gate_norm",
    )(hdir, hdir, proj, proj, g_head.reshape(1, ML_INNER))


def _dwconv3_tile(x, w_ref, row_len, tm):
    r = lax.broadcasted_iota(jnp.int32, (tm, 1), 0) & (row_len - 1)
    prev = jnp.where(r == 0, 0.0, pltpu.roll(x, 1, axis=0))
    nxt = jnp.where(r == row_len - 1, 0.0, pltpu.roll(x, tm - 1, axis=0))
    return prev * w_ref[0:1, :] + x * w_ref[1:2, :] + nxt * w_ref[2:3, :]


def _sc_kernel(a_ref, wu_ref, wb_ref, wc_ref, wz_ref, cw_ref, y_ref, wbuf, *, tm):
    i = pl.program_id(1)

    @pl.when(i == 0)
    def _():
        for g, w_ref in enumerate((wu_ref, wb_ref, wc_ref, wz_ref)):
            wbuf[g] = w_ref[...].astype(BF16)

    a = a_ref[...]
    u, b, c, z = (_dot(a, wbuf[g]) for g in range(4))
    conv = _dwconv3_tile(c * u, cw_ref, _conv_row_len(i * tm), tm)
    y_ref[...] = (b * conv * _silu(z)).astype(BF16)


def _shortconv(a, w_in, layer, w_conv):
    tm, tn = 1024, 256
    k = a.shape[1]
    nb = SC_INNER // tn
    w_specs = [pl.BlockSpec((None, k, tn), functools.partial(lambda j, i, g: (layer, 0, g * nb + j), g=g))
               for g in range(4)]
    return pl.pallas_call(
        functools.partial(_sc_kernel, tm=tm),
        grid=(nb, T_ALL // tm),
        in_specs=[pl.BlockSpec((tm, k), lambda j, i: (i, 0)), *w_specs,
                  pl.BlockSpec((None, 3, tn), lambda j, i: (layer, 0, j))],
        out_specs=pl.BlockSpec((tm, tn), lambda j, i: (i, j)),
        out_shape=jax.ShapeDtypeStruct((T_ALL, SC_INNER), BF16),
        scratch_shapes=[pltpu.VMEM((4, k, tn), BF16)],
        compiler_params=_cparams(2),
        name="shortconv_mixer",
    )(a, w_in, w_in, w_in, w_in, w_conv)


def _gdn_qkv_kernel(a_ref, w_ref, cw_ref, o_ref, wb_ref, *, tm, tn):
    j = pl.program_id(0)
    i = pl.program_id(1)

    @pl.when(i == 0)
    def _():
        wb_ref[...] = w_ref[...].astype(BF16)

    y = _silu(_dwconv3_tile(_dot(a_ref[...], wb_ref[...]), cw_ref, _conv_row_len(i * tm), tm))
    q_blocks = GD_QK // tn
    is_qk = j < 2 * q_blocks
    q_scale = jnp.where(j < q_blocks, GD_DK ** -0.5, 1.0)
    for hh in range(tn // GD_DK):
        yh = y[:, hh * GD_DK:(hh + 1) * GD_DK]
        r = lax.rsqrt(jnp.sum(yh * yh, axis=-1, keepdims=True) + EPS) * q_scale
        o_ref[:, hh * GD_DK:(hh + 1) * GD_DK] = (yh * jnp.where(is_qk, r, 1.0)).astype(o_ref.dtype)


def _gdn_qkv(a, w_in, layer, w_conv):
    tm, tn = 1024, 1024
    k = a.shape[1]
    return pl.pallas_call(
        functools.partial(_gdn_qkv_kernel, tm=tm, tn=tn),
        grid=(GD_CONV_CH // tn, T_ALL // tm),
        in_specs=[
            pl.BlockSpec((tm, k), lambda j, i: (i, 0)),
            pl.BlockSpec((None, k, tn), lambda j, i: (layer, 0, j)),
            pl.BlockSpec((None, 3, tn), lambda j, i: (layer, 0, j)),
        ],
        out_specs=pl.BlockSpec((tm, tn), lambda j, i: (i, j)),
        out_shape=jax.ShapeDtypeStruct((T_ALL, GD_CONV_CH), BF16),
        scratch_shapes=[pltpu.VMEM((k, tn), BF16)],
        compiler_params=_cparams(2),
        name="gdn_qkv_projection",
    )(a, w_in, w_conv)


def _gdn_kernel(q_ref, k_ref, v_ref, gc_ref, gr_ref, pc_ref, pr_ref, s0_ref,
                o_ref, sout_ref, s_s):
    L = GD_CHUNK
    d = pl.program_id(0)
    j = GD_SLOTS.slot(d, pl.program_id(2))
    is_ctx, is_head, is_tail = GD_SLOTS.flags(d, j)

    @pl.when(jnp.logical_and(is_head, is_ctx))
    def _():
        s_s[...] = jnp.zeros_like(s_s)

    @pl.when(jnp.logical_and(is_head, jnp.logical_not(is_ctx)))
    def _():
        s_s[...] = s0_ref[...]

    incl, _ = GD_SLOTS.mask(d)
    tri = incl.astype(F32)
    nh = GD_GV
    gc = gc_ref[...]
    gr = gr_ref[...]
    g_c = -jnp.exp(pc_ref[0:1, :]) * _softplus(gc[:, 0:nh] + pc_ref[1:2, :])
    G_c = _dot(tri, g_c, precision=HIGHEST)
    beta_c = _sigmoid(gc[:, nh:2 * nh])
    g_r = -jnp.exp(pr_ref[:, 0:1]) * _softplus(gr[0:nh, :] + pr_ref[:, 1:2])
    G_r = _dot_nt(g_r, tri, precision=HIGHEST)
    Gl_all = jnp.where(d == 0, G_c[L - 1:L, :], G_c[0:1, :])

    pairs = range(GD_GQ)
    lane_lo = lax.broadcasted_iota(jnp.int32, (L, 2 * L), 1) < L
    t_i = lax.broadcasted_iota(jnp.int32, (L, 2 * L), 0)
    s_i = lax.broadcasted_iota(jnp.int32, (L, 2 * L), 1) & (L - 1)
    diff = (t_i - s_i) * (1 - 2 * d)
    incl2, strict2 = diff >= 0, diff > 0

    def block_diag(x):
        zero = jnp.zeros_like(x)
        return jnp.concatenate([jnp.where(lane_lo, x, zero), jnp.where(lane_lo, zero, x)], axis=0)

    def stack_cols(a, p):
        return jnp.concatenate([a[:, 2 * p:2 * p + 1], a[:, 2 * p + 1:2 * p + 2]], axis=0)

    qb, k32, kkqk = [], [], []
    for p in pairs:
        kb = k_ref[:, p * GD_DK:(p + 1) * GD_DK]
        qb.append(q_ref[:, p * GD_DK:(p + 1) * GD_DK])
        k32.append(kb.astype(F32))
        kkqk.append(_dot_nt(jnp.concatenate([kb, qb[p]], axis=0), jnp.concatenate([kb, kb], axis=0)))

    neg_a, p_mat, rhs, eG = [], [], [], []
    for p in pairs:
        Gc2 = jnp.where(lane_lo, G_c[:, 2 * p:2 * p + 1], G_c[:, 2 * p + 1:2 * p + 2])
        Gr2 = jnp.concatenate([G_r[2 * p:2 * p + 1, :], G_r[2 * p + 1:2 * p + 2, :]], axis=1)
        beta2 = jnp.where(lane_lo, beta_c[:, 2 * p:2 * p + 1], beta_c[:, 2 * p + 1:2 * p + 2])
        decay = jnp.where(incl2, jnp.exp(jnp.where(incl2, Gc2 - Gr2, 0.0)), 0.0)
        neg_a.append(jnp.where(strict2, -(beta2 * kkqk[p][0:L] * decay), 0.0))
        p_mat.append((kkqk[p][L:2 * L] * decay).astype(BF16))
        beta_s = stack_cols(beta_c, p)
        eG.append(jnp.exp(stack_cols(G_c, p)))
        k2 = jnp.concatenate([k32[p], k32[p]], axis=0)
        v2 = v_ref[:, 2 * p * GD_DV:(2 * p + 2) * GD_DV].astype(F32)
        v2 = jnp.concatenate([v2[:, 0:GD_DV], v2[:, GD_DV:2 * GD_DV]], axis=0)
        rhs.append(jnp.concatenate([(beta_s * eG[p]) * k2, beta_s * v2], axis=-1))

    n_hi = [a.astype(BF16) for a in neg_a]
    n_bd = [block_diag(x) for x in n_hi]
    m_inv = list(neg_a)
    p32 = [_dot(x, b) for x, b in zip(n_hi, n_bd)]
    for k in range(1, 6):
        pw = [x.astype(BF16) for x in p32]
        pw_bd = [block_diag(x) for x in pw]
        if k < 5:
            both = [_dot(jnp.concatenate([x, m.astype(BF16)], axis=0), b) for x, m, b in zip(pw, m_inv, pw_bd)]
            m_inv = [m + x + b[L:2 * L] for m, x, b in zip(m_inv, p32, both)]
            p32 = [b[0:L] for b in both]
        else:
            m_inv = [m + x + _dot(m.astype(BF16), b) for m, x, b in zip(m_inv, p32, pw_bd)]
    m_bd = [block_diag(m.astype(BF16)) for m in m_inv]
    x1 = [r + _dot(m, r.astype(BF16)) for m, r in zip(m_bd, rhs)]
    res = []
    for r, x, a, ah, ah_bd in zip(rhs, x1, neg_a, n_hi, n_bd):
        al_bd = block_diag((a - ah.astype(F32)).astype(BF16))
        xh = x.astype(BF16)
        xl = (x - xh.astype(F32)).astype(BF16)
        hi = _dot(jnp.concatenate([ah_bd, al_bd], axis=0), xh)
        res.append((r - x) + (hi[0:2 * L] + hi[2 * L:4 * L] + _dot(ah_bd, xl)))
    sol = [x + r + _dot(m, r.astype(BF16)) for x, r, m in zip(x1, res, m_bd)]

    heads = range(nh)
    rows = [slice((hv % 2) * L, (hv % 2 + 1) * L) for hv in heads]
    S = [s_s[hv] for hv in heads]
    Sb = [x.astype(BF16) for x in S]
    wq = [_dot(jnp.concatenate([sol[hv // 2][rows[hv], 0:GD_DK].astype(BF16), qb[hv // 2]], axis=0), Sb[hv])
          for hv in heads]
    ub = [jnp.concatenate([sol[p][rows[2 * p + r], GD_DK:GD_DK + GD_DV] - wq[2 * p + r][0:L] for r in range(2)],
                          axis=0).astype(BF16) for p in pairs]
    pu = [_dot(block_diag(p_mat[p]), ub[p]) for p in pairs]
    kdec = [(jnp.exp(Gl_all[:, hv:hv + 1] - G_c[:, hv:hv + 1]) * k32[hv // 2]).astype(BF16) for hv in heads]
    ku = [_dot_tn(kdec[hv], ub[hv // 2][rows[hv]]) for hv in heads]
    for hv in heads:
        o = eG[hv // 2][rows[hv]] * wq[hv][L:2 * L] + pu[hv // 2][rows[hv]]
        o_ref[:, hv * GD_DV:(hv + 1) * GD_DV] = o.astype(o_ref.dtype)
        s_s[hv] = jnp.exp(Gl_all[:, hv:hv + 1]) * S[hv] + ku[hv]

    @pl.when(jnp.logical_and(is_tail, is_ctx))
    def _():
        sout_ref[...] = s_s[...]


def _gdn_scan(qkv, ab, a_log, dt_bias, cache_s, j_layer):
    HV, nh, L, S = GD_V_HEADS, GD_GV, GD_CHUNK, GD_SLOTS
    a_d = ab[:, 0:2 * HV].reshape(T_ALL, 2, GD_GROUPS, nh)
    b_d = ab[:, 2 * HV:4 * HV].reshape(T_ALL, 2, GD_GROUPS, nh)
    gcol = jnp.concatenate([a_d, b_d], axis=-1).reshape(S.n, L, 2, GD_GROUPS, 2 * nh)
    gcol = gcol.transpose(2, 3, 0, 1, 4)
    grow = gcol.transpose(0, 1, 2, 4, 3)
    par = jnp.stack([a_log.astype(F32), dt_bias.astype(F32)], axis=1)
    pcol = par.reshape(2, 2, GD_GROUPS, nh).transpose(0, 2, 1, 3)
    prow = pcol.transpose(0, 1, 3, 2)

    qb = GD_GQ * GD_DK
    vb = nh * GD_DV
    return pl.pallas_call(
        _gdn_kernel,
        grid=(2, GD_GROUPS, S.n),
        in_specs=[
            pl.BlockSpec((L, qb), lambda d, p, g: (S.slot(d, g), p)),
            pl.BlockSpec((L, qb), lambda d, p, g: (S.slot(d, g), GD_GROUPS + p)),
            pl.BlockSpec((L, vb), lambda d, p, g: (S.slot(d, g), GD_GROUPS + p)),
            pl.BlockSpec((None, None, None, L, 2 * nh), lambda d, p, g: (d, p, S.slot(d, g), 0, 0)),
            pl.BlockSpec((None, None, None, 2 * nh, L), lambda d, p, g: (d, p, S.slot(d, g), 0, 0)),
            pl.BlockSpec((None, None, 2, nh), lambda d, p, g: (d, p, 0, 0)),
            pl.BlockSpec((None, None, nh, 2), lambda d, p, g: (d, p, 0, 0)),
            pl.BlockSpec((None, None, None, nh, GD_DK, GD_DV),
                         lambda d, p, g: (S.lat_batch(d, g), j_layer, d, p, 0, 0)),
        ],
        out_specs=[
            pl.BlockSpec((None, L, vb), lambda d, p, g: (d, S.slot(d, g), p)),
            pl.BlockSpec((None, None, nh, GD_DK, GD_DV), lambda d, p, g: (S.ctx_seq(d, g), d, p, 0, 0)),
        ],
        out_shape=[
            jax.ShapeDtypeStruct((2, T_ALL, GD_INNER), BF16),
            jax.ShapeDtypeStruct((BATCH, 2, HV, GD_DK, GD_DV), F32),
        ],
        scratch_shapes=[pltpu.VMEM((nh, GD_DK, GD_DV), F32)],
        compiler_params=_cparams(3),
        name="gdn_scan",
    )(qkv, qkv, qkv, gcol, grow, pcol, prow, cache_s)


def _gdn_post_kernel(of_ref, ob_ref, z_ref, g_ref, y_ref, *, tn):
    g = g_ref[...]
    for hh in range(tn // GD_DV):
        sl = slice(hh * GD_DV, (hh + 1) * GD_DV)
        o = of_ref[:, sl].astype(F32) + ob_ref[:, sl].astype(F32)
        r = lax.rsqrt(jnp.mean(o * o, axis=-1, keepdims=True) + EPS)
        y_ref[:, sl] = ((o * r * g) * _silu(z_ref[:, sl].astype(F32))).astype(BF16)


def _gdn_post(odir, z, g_norm):
    tm, tn = 1024, 512
    return pl.pallas_call(
        functools.partial(_gdn_post_kernel, tn=tn),
        grid=(T_ALL // tm, GD_INNER // tn),
        in_specs=[
            pl.BlockSpec((None, tm, tn), lambda i, j: (0, i, j)),
            pl.BlockSpec((None, tm, tn), lambda i, j: (1, i, j)),
            pl.BlockSpec((tm, tn), lambda i, j: (i, j)),
            pl.BlockSpec((1, GD_DV), lambda i, j: (0, 0)),
        ],
        out_specs=pl.BlockSpec((tm, tn), lambda i, j: (i, j)),
        out_shape=jax.ShapeDtypeStruct((T_ALL, GD_INNER), BF16),
        compiler_params=_cparams(2),
        name="gdn_gate_norm",
    )(odir, odir, z, g_norm.reshape(1, GD_DV))


def kernel(x_prompt, x_sample, c, cache_ml_C, cache_ml_n, cache_ml_m, cache_gd_S, c_ctx, w_ada, b_ada, g_norm,
           w_ml_in, b_ml_gate, g_ml_head, w_ml_out, w_sc_in, w_sc_conv, w_sc_out, w_gd_in, w_gd_conv, gd_A_log,
           gd_dt_bias, g_gd_norm, w_gd_out, g_final):
    xs = (x_prompt.reshape(T_CTX, D_MODEL), x_sample.reshape(T_LAT, D_MODEL))
    cond = jnp.concatenate([c_ctx[None, :], c, jnp.zeros((N_COND - 1 - DEC_BATCH, D_MODEL), F32)], axis=0)
    mod = _modulation(cond, w_ada, b_ada)
    w_ml_nk = jnp.swapaxes(w_ml_in, 1, 2)

    state_ml_c = None
    ml_n, ml_m, gd_s = [], [], []
    for l in range(DEPTH):
        shift = mod[l, :, 0:D_MODEL]
        scale = mod[l, :, D_MODEL:2 * D_MODEL]
        gate = mod[l, :, 2 * D_MODEL:3 * D_MODEL]
        h = _norm_mod(xs, g_norm[l], shift, scale)
        j = l // 3
        kind = l % 3
        if kind == 0:
            proj = _project(h, w_ml_nk, j, 0, ML_MAIN, 1024, BF16, w_is_nk=True)
            gates = _project(h, w_ml_nk, j, ML_MAIN, 4 * ML_HEADS, 4 * ML_HEADS, F32, w_is_nk=True)
            hdir, state_ml_c, n_fin, m_fin = _mlstm_scan(proj, gates, b_ml_gate[j], cache_ml_C, cache_ml_n,
                                                         cache_ml_m, j, state_ml_c)
            ml_n.append(n_fin)
            ml_m.append(m_fin[..., 0])
            y = _mlstm_post(hdir, proj, g_ml_head[j])
            xs = (_project_residual(y, w_ml_out, j, xs, gate),)
        elif kind == 1:
            y = _shortconv(h, w_sc_in, j, w_sc_conv)
            xs = (_project_residual(y, w_sc_out, j, xs, gate),)
        else:
            qkv = _gdn_qkv(h, w_gd_in, j, w_gd_conv)
            z = _project(h, w_gd_in, j, GD_CONV_CH, GD_INNER, 1024, BF16)
            ab = _project(h, w_gd_in, j, GD_MAIN, 4 * GD_V_HEADS, 4 * GD_V_HEADS, F32)
            odir, s_fin = _gdn_scan(qkv, ab, gd_A_log[j], gd_dt_bias[j], cache_gd_S, j)
            gd_s.append(s_fin)
            y = _gdn_post(odir, z, g_gd_norm[j])
            xs = (_project_residual(y, w_gd_out, j, xs, gate),)

    y_ctx, y_lat = _final_norm(xs[0], g_final)
    y_prompt = y_ctx.reshape(BATCH, SEQ, D_MODEL)
    y_sample = y_lat.reshape(DEC_BATCH, DEC_SEQ, D_MODEL)
    state_ml_n = jnp.stack(ml_n, axis=1)
    state_ml_m = jnp.stack(ml_m, axis=1)
    state_gd_s = jnp.stack(gd_s, axis=1)
    return (y_prompt, y_sample, state_ml_c, state_ml_n, state_ml_m, state_gd_s)
```

```python
import functools

import jax
import jax.numpy as jnp
from jax import lax
from jax.experimental import pallas as pl
from jax.experimental.pallas import tpu as pltpu

F32 = jnp.float32
BF16 = jnp.bfloat16
HIGHEST = lax.Precision.HIGHEST

D_MODEL = 2048
BATCH = 16
SEQ = 256
DEPTH = 4
DEC_BATCH = 2
DEC_SEQ = 2048
GRID_W = 64
EPS = 1e-6

T_CTX = BATCH * SEQ
T_LAT = DEC_BATCH * DEC_SEQ
T_ALL = T_CTX + T_LAT
N_COND = 8

ML_HEADS = 8
ML_DK = 256
ML_DV = 512
ML_QK = ML_HEADS * ML_DK
ML_INNER = ML_HEADS * ML_DV
ML_MAIN = 2 * ML_QK + 3 * ML_INNER
N_ML = 2
ML_CHUNK = SEQ
ML_LAT_CHUNKS = DEC_SEQ // ML_CHUNK
ML_LAT_SLOTS = DEC_BATCH * ML_LAT_CHUNKS
ML_CBLK = 256

SC_INNER = 2 * D_MODEL

GD_DK = 128
GD_DV = 128
GD_QK_HEADS = 16
GD_V_HEADS = 32
GD_QK = GD_QK_HEADS * GD_DK
GD_INNER = GD_V_HEADS * GD_DV
GD_CONV_CH = 2 * GD_QK + GD_INNER
GD_MAIN = GD_CONV_CH + GD_INNER
GD_GROUPS = 1
GD_GV = GD_V_HEADS // GD_GROUPS
GD_GQ = GD_QK_HEADS // GD_GROUPS
GD_CHUNK = 64

LANE = 128
BF16_ROWS = 16
VMEM_LIMIT = 56 * 1024 * 1024


def _cparams(n_axes):
    return pltpu.CompilerParams(dimension_semantics=("arbitrary",) * n_axes, vmem_limit_bytes=VMEM_LIMIT)


def _sigmoid(x):
    return 0.5 * jnp.tanh(0.5 * x) + 0.5


def _silu(x):
    h = 0.5 * x
    return h + h * jnp.tanh(h)


def _softplus(x):
    return jnp.maximum(x, 0.0) + jnp.log1p(jnp.exp(-jnp.abs(x)))


def _log_sigmoid(x):
    return -_softplus(-x)


def _cond_group(row0):
    return jnp.where(row0 < T_CTX, 0, 1 + (row0 - T_CTX) // DEC_SEQ)


def _conv_row_len(row0):
    return jnp.where(row0 < T_CTX, SEQ, GRID_W)


def _dot_nt(a, b, **kw):
    return lax.dot_general(a, b, (((1,), (1,)), ((), ())), preferred_element_type=F32, **kw)


def _dot_tn(a, b, **kw):
    return lax.dot_general(a, b, (((0,), (0,)), ((), ())), preferred_element_type=F32, **kw)


def _dot(a, b, **kw):
    return jnp.dot(a, b, preferred_element_type=F32, **kw)


def _mod_kernel(s_ref, w_ref, b_ref, o_ref):
    a = _silu(s_ref[...]).astype(BF16)
    o_ref[...] = _dot(a, w_ref[...].astype(BF16)) + b_ref[...]


def _modulation(cond, w_ada, b_ada):
    tn = 1024
    n = 3 * D_MODEL
    return pl.pallas_call(
        _mod_kernel,
        grid=(DEPTH, n // tn),
        in_specs=[
            pl.BlockSpec((N_COND, D_MODEL), lambda l, j: (0, 0)),
            pl.BlockSpec((None, D_MODEL, tn), lambda l, j: (l, 0, j)),
            pl.BlockSpec((None, 1, tn), lambda l, j: (l, 0, j)),
        ],
        out_specs=pl.BlockSpec((None, N_COND, tn), lambda l, j: (l, 0, j)),
        out_shape=jax.ShapeDtypeStruct((DEPTH, N_COND, n), F32),
        compiler_params=_cparams(2),
        name="adaln_modulation",
    )(cond, w_ada, b_ada.reshape(DEPTH, 1, n))


def _split_specs(block, index_map, tm):
    n_ctx = T_CTX // tm

    def ctx_map(*idx):
        row, *rest = index_map(*idx)
        return (jnp.minimum(row, n_ctx - 1), *rest)

    def lat_map(*idx):
        row, *rest = index_map(*idx)
        return (jnp.maximum(row - n_ctx, 0), *rest)

    return [pl.BlockSpec(block, ctx_map), pl.BlockSpec(block, lat_map)]


def _token_tile(refs, row0):
    if len(refs) == 1:
        return refs[0][...]
    return jnp.where(row0 < T_CTX, refs[0][...], refs[1][...])


def _norm_mod_kernel(*refs, tm):
    *x_refs, g_ref, shift_ref, scale_ref, o_ref = refs
    row0 = pl.program_id(0) * tm
    grp = _cond_group(row0)
    x = _token_tile(x_refs, row0)
    r = lax.rsqrt(jnp.mean(x * x, axis=-1, keepdims=True) + EPS)
    xn = x * r * g_ref[...]
    sh = shift_ref[pl.ds(grp, 1), :]
    sc = scale_ref[pl.ds(grp, 1), :]
    o_ref[...] = (xn * (1.0 + sc) + sh).astype(BF16)


def _norm_mod(xs, g, shift, scale):
    tm = 512
    x_map = lambda i: (i, 0)
    x_specs = [pl.BlockSpec((tm, D_MODEL), x_map)] if len(xs) == 1 else _split_specs((tm, D_MODEL), x_map, tm)
    return pl.pallas_call(
        functools.partial(_norm_mod_kernel, tm=tm),
        grid=(T_ALL // tm,),
        in_specs=[
            *x_specs,
            pl.BlockSpec((1, D_MODEL), lambda i: (0, 0)),
            pl.BlockSpec((N_COND, D_MODEL), lambda i: (0, 0)),
            pl.BlockSpec((N_COND, D_MODEL), lambda i: (0, 0)),
        ],
        out_specs=pl.BlockSpec((tm, D_MODEL), lambda i: (i, 0)),
        out_shape=jax.ShapeDtypeStruct((T_ALL, D_MODEL), BF16),
        compiler_params=_cparams(1),
        name="norm_modulate",
    )(*xs, g.reshape(1, D_MODEL), shift, scale)


def _final_norm_kernel(x_ref, g_ref, ctx_ref, lat_ref, *, ctx_tiles):
    x = x_ref[...]
    r = lax.rsqrt(jnp.mean(x * x, axis=-1, keepdims=True) + EPS)
    y = x * r * g_ref[...]
    i = pl.program_id(0)

    @pl.when(i < ctx_tiles)
    def _():
        ctx_ref[...] = y

    @pl.when(i >= ctx_tiles)
    def _():
        lat_ref[...] = y


def _final_norm(x, g):
    tm = 512
    ctx_tiles = T_CTX // tm
    return pl.pallas_call(
        functools.partial(_final_norm_kernel, ctx_tiles=ctx_tiles),
        grid=(T_ALL // tm,),
        in_specs=[
            pl.BlockSpec((tm, D_MODEL), lambda i: (i, 0)),
            pl.BlockSpec((1, D_MODEL), lambda i: (0, 0)),
        ],
        out_specs=[
            pl.BlockSpec((tm, D_MODEL), lambda i: (jnp.minimum(i, ctx_tiles - 1), 0)),
            pl.BlockSpec((tm, D_MODEL), lambda i: (jnp.maximum(i - ctx_tiles, 0), 0)),
        ],
        out_shape=[
            jax.ShapeDtypeStruct((T_CTX, D_MODEL), F32),
            jax.ShapeDtypeStruct((T_LAT, D_MODEL), F32),
        ],
        compiler_params=_cparams(1),
        name="final_norm",
    )(x, g.reshape(1, D_MODEL))


def _proj_kernel(a_ref, w_ref, o_ref, wb_ref, *, w_is_nk):
    @pl.when(pl.program_id(1) == 0)
    def _():
        wb_ref[...] = w_ref[...].astype(BF16)

    dot = _dot_nt if w_is_nk else _dot
    o_ref[...] = dot(a_ref[...], wb_ref[...]).astype(o_ref.dtype)


def _project(a, w, layer, col0, n_cols, tn, out_dtype, w_is_nk=False, tm=1024):
    k = a.shape[1]
    j0 = col0 // tn
    if w_is_nk:
        w_spec = pl.BlockSpec((None, tn, k), lambda j, i: (layer, j0 + j, 0))
        w_tile = (tn, k)
    else:
        w_spec = pl.BlockSpec((None, k, tn), lambda j, i: (layer, 0, j0 + j))
        w_tile = (k, tn)
    return pl.pallas_call(
        functools.partial(_proj_kernel, w_is_nk=w_is_nk),
        grid=(n_cols // tn, T_ALL // tm),
        in_specs=[pl.BlockSpec((tm, k), lambda j, i: (i, 0)), w_spec],
        out_specs=pl.BlockSpec((tm, tn), lambda j, i: (i, j)),
        out_shape=jax.ShapeDtypeStruct((T_ALL, n_cols), out_dtype),
        scratch_shapes=[pltpu.VMEM(w_tile, BF16)],
        compiler_params=_cparams(2),
        name="in_projection",
    )(a, w)


def _proj_res_kernel(a_ref, w_ref, *refs, tm):
    *x_refs, gate_ref, o_ref, wb_ref = refs
    i = pl.program_id(1)

    @pl.when(i == 0)
    def _():
        wb_ref[...] = w_ref[...].astype(BF16)

    gate = gate_ref[pl.ds(_cond_group(i * tm), 1), :]
    o_ref[...] = _token_tile(x_refs, i * tm) + gate * _dot(a_ref[...], wb_ref[...])


def _project_residual(a, w, layer, xs, gate):
    k = a.shape[1]
    tm, tn = 1024, 512
    x_map = lambda j, i: (i, j)
    x_specs = [pl.BlockSpec((tm, tn), x_map)] if len(xs) == 1 else _split_specs((tm, tn), x_map, tm)
    return pl.pallas_call(
        functools.partial(_proj_res_kernel, tm=tm),
        grid=(D_MODEL // tn, T_ALL // tm),
        in_specs=[
            pl.BlockSpec((tm, k), lambda j, i: (i, 0)),
            pl.BlockSpec((None, k, tn), lambda j, i: (layer, 0, j)),
            *x_specs,
            pl.BlockSpec((N_COND, tn), lambda j, i: (0, j)),
        ],
        out_specs=pl.BlockSpec((tm, tn), lambda j, i: (i, j)),
        out_shape=jax.ShapeDtypeStruct((T_ALL, D_MODEL), F32),
        scratch_shapes=[pltpu.VMEM((k, tn), BF16)],
        compiler_params=_cparams(2),
        name="out_projection_residual",
    )(a, w, *xs, gate)


def _chunk_tri(tm, chunk):
    r = lax.broadcasted_iota(jnp.int32, (tm, tm), 0)
    c = lax.broadcasted_iota(jnp.int32, (tm, tm), 1)
    same = (r // chunk) == (c // chunk)
    lower = jnp.logical_and(same, c <= r).astype(F32)
    upper = jnp.logical_and(same, c >= r).astype(F32)
    return lower, upper


def _ml_gate_kernel(a_ref, w_ref, b_ref, o_ref, *, tm):
    H = ML_HEADS
    x = _dot_nt(a_ref[...], w_ref[...].astype(BF16)) + b_ref[...]
    lower, upper = _chunk_tri(tm, ML_CHUNK)
    lf = _log_sigmoid(x[:, 2 * H:4 * H])
    o_ref[:, 0:2 * H] = x[:, 0:2 * H]
    o_ref[:, 2 * H:3 * H] = _dot(lower, lf[:, 0:H], precision=HIGHEST)
    o_ref[:, 3 * H:4 * H] = _dot(upper, lf[:, H:2 * H], precision=HIGHEST)


def _ml_gates(a, w_nk, layer, b_gate):
    tm, n = 512, 4 * ML_HEADS
    k = a.shape[1]
    return pl.pallas_call(
        functools.partial(_ml_gate_kernel, tm=tm),
        grid=(T_ALL // tm,),
        in_specs=[
            pl.BlockSpec((tm, k), lambda i: (i, 0)),
            pl.BlockSpec((None, n, k), lambda i: (layer, ML_MAIN // n, 0)),
            pl.BlockSpec((1, n), lambda i: (0, 0)),
        ],
        out_specs=pl.BlockSpec((tm, n), lambda i: (i, 0)),
        out_shape=jax.ShapeDtypeStruct((T_ALL, n), F32),
        compiler_params=_cparams(1),
        name="mlstm_gates",
    )(a, w_nk, b_gate.reshape(1, n))


def _gd_gate_kernel(a_ref, w_ref, p_ref, o_ref, *, tm):
    HV = GD_V_HEADS
    x = _dot(a_ref[...], w_ref[...].astype(BF16))
    lower, upper = _chunk_tri(tm, GD_CHUNK)
    g = -jnp.exp(p_ref[0:1, :]) * _softplus(x[:, 0:2 * HV] + p_ref[1:2, :])
    o_ref[:, 0:HV] = _dot(lower, g[:, 0:HV], precision=HIGHEST)
    o_ref[:, HV:2 * HV] = _dot(upper, g[:, HV:2 * HV], precision=HIGHEST)
    o_ref[:, 2 * HV:4 * HV] = _sigmoid(x[:, 2 * HV:4 * HV])


def _gd_gates(a, w_in, layer, a_log, dt_bias):
    tm, n = 512, 4 * GD_V_HEADS
    k = a.shape[1]
    par = jnp.stack([a_log.astype(F32).reshape(-1), dt_bias.astype(F32).reshape(-1)], axis=0)
    return pl.pallas_call(
        functools.partial(_gd_gate_kernel, tm=tm),
        grid=(T_ALL // tm,),
        in_specs=[
            pl.BlockSpec((tm, k), lambda i: (i, 0)),
            pl.BlockSpec((None, k, n), lambda i: (layer, 0, GD_MAIN // n)),
            pl.BlockSpec((2, n // 2), lambda i: (0, 0)),
        ],
        out_specs=pl.BlockSpec((tm, n), lambda i: (i, 0)),
        out_shape=jax.ShapeDtypeStruct((T_ALL, n), F32),
        compiler_params=_cparams(1),
        name="gdn_gates",
    )(a, w_in, par)


class _Slots:
    def __init__(self, chunk):
        self.chunk = chunk
        self.n = T_ALL // chunk
        self.ctx = T_CTX // chunk
        self.ctx_chunks = SEQ // chunk
        self.lat_chunks = DEC_SEQ // chunk

    def slot(self, d, g):
        return g + d * (self.n - 1 - 2 * g)

    def seq(self, j):
        return jnp.where(j < self.ctx, j // self.ctx_chunks, BATCH + (j - self.ctx) // self.lat_chunks)

    def flags(self, d, j):
        is_ctx = j < self.ctx
        pos = jnp.where(is_ctx, j % self.ctx_chunks, (j - self.ctx) % self.lat_chunks)
        n_chunks = jnp.where(is_ctx, self.ctx_chunks, self.lat_chunks)
        head_pos = jnp.where(d == 0, 0, n_chunks - 1)
        tail_pos = jnp.where(d == 0, n_chunks - 1, 0)
        return is_ctx, pos == head_pos, pos == tail_pos

    def mask(self, d):
        t_i = lax.broadcasted_iota(jnp.int32, (self.chunk, self.chunk), 0)
        s_i = lax.broadcasted_iota(jnp.int32, (self.chunk, self.chunk), 1)
        diff = (t_i - s_i) * (1 - 2 * d)
        return diff >= 0, diff > 0

    def lat_batch(self, d, g):
        return jnp.clip(self.seq(self.slot(d, g)) - BATCH, 0, DEC_BATCH - 1)

    def ctx_seq(self, d, g):
        return jnp.minimum(self.seq(self.slot(d, g)), BATCH - 1)


GD_SLOTS = _Slots(GD_CHUNK)


def _mlstm_kernel(*refs, recurrent, n_alias):
    q_ref, k_ref, v_ref, gc_ref, gr_ref, *rest = refs
    if recurrent:
        c0_ref, n0_ref, m0_ref, *rest = rest
    rest = rest[n_alias:]
    if recurrent:
        h_ref, c_s, n_s, m_s = rest
    else:
        h_ref, cout_ref, nout_ref, mout_ref = rest
    L = ML_CHUNK
    d = pl.program_id(0)

    if recurrent:
        pos = _lat_slot(d, pl.program_id(1)) % ML_LAT_CHUNKS

        @pl.when(pos == jnp.where(d == 0, 0, ML_LAT_CHUNKS - 1))
        def _():
            c_s[...] = c0_ref[...]
            n_s[...] = n0_ref[...]
            m_s[...] = m0_ref[...]

    t_i = lax.broadcasted_iota(jnp.int32, (L, L), 0)
    s_i = lax.broadcasted_iota(jnp.int32, (L, L), 1)
    mask = (t_i - s_i) * (1 - 2 * d) >= 0
    gc = gc_ref[...]
    gr = gr_ref[...]
    i_c = gc[:, 0:ML_HEADS]
    b_c = gc[:, ML_HEADS:2 * ML_HEADS]
    i_r = gr[0:ML_HEADS, :]
    b_r = gr[ML_HEADS:2 * ML_HEADS, :]
    bl_all = jnp.where(d == 0, b_c[L - 1:L, :], b_c[0:1, :])

    qbs = [q_ref[:, h * ML_DK:(h + 1) * ML_DK] * (ML_DK ** -0.5) for h in range(ML_HEADS)]
    a_all, m_new_all, dec_all, z_all, mt_all, sc_all = [], [], [], [], [], []
    for h in range(ML_HEADS):
        bcol = b_c[:, h:h + 1]
        m = m_s[h:h + 1, 0:1] if recurrent else 0.0
        z = jnp.where(mask, i_r[h:h + 1, :] - b_r[h:h + 1, :], -jnp.inf)
        inter = bcol + m
        mt = jnp.maximum(inter, bcol + jnp.max(z, axis=-1, keepdims=True))
        z_all.append(bcol - mt)
        mt_all.append(mt)
        bl = bl_all[:, h:h + 1]
        a = bl - bcol + i_c[:, h:h + 1]
        m_new = jnp.maximum(bl + m, jnp.max(a, axis=0, keepdims=True))
        a_all.append(a)
        m_new_all.append(m_new)
        if recurrent:
            sc_all.append(jnp.exp(inter - mt))
            dec_all.append(jnp.exp(bl + m - m_new))
    qkn = []
    for h in range(ML_HEADS):
        keys = k_ref[:, h * ML_DK:(h + 1) * ML_DK]
        if recurrent:
            nb = jnp.broadcast_to(n_s[h:h + 1, :], (BF16_ROWS, ML_DK)).astype(BF16)
            keys = jnp.concatenate([keys, nb], axis=0)
        qkn.append(_dot_nt(qbs[h], keys))
    lhs, rinv = [], []
    for h in range(ML_HEADS):
        z = jnp.where(mask, i_r[h:h + 1, :] - b_r[h:h + 1, :], -jnp.inf)
        s = qkn[h][:, 0:L] * jnp.exp(z + z_all[h])
        den = jnp.sum(s, axis=-1, keepdims=True)
        if recurrent:
            den = sc_all[h] * qkn[h][:, L:L + 1] + den
            lhs.append(jnp.concatenate([(qbs[h].astype(F32) * sc_all[h]).astype(BF16), s.astype(BF16)], axis=1))
        else:
            lhs.append(s.astype(BF16))
        rinv.append(1.0 / jnp.maximum(jnp.abs(den), jnp.exp(-mt_all[h])))
    for h in range(ML_HEADS):
        w = v_ref[:, h * ML_DV:(h + 1) * ML_DV]
        if recurrent:
            w = jnp.concatenate([c_s[h].astype(BF16), w], axis=0)
        h_ref[:, h * ML_DV:(h + 1) * ML_DV] = (_dot(lhs[h], w) * rinv[h]).astype(h_ref.dtype)

    for h in range(ML_HEADS):
        wk = jnp.exp(a_all[h] - m_new_all[h]) * k_ref[:, h * ML_DK:(h + 1) * ML_DK].astype(F32)
        wkb = wk.astype(BF16)
        n_add = jnp.sum(wk, axis=0, keepdims=True)
        m_row = jnp.broadcast_to(m_new_all[h], (1, LANE))
        for cb in range(ML_DV // ML_CBLK):
            sl = slice(h * ML_DV + cb * ML_CBLK, h * ML_DV + (cb + 1) * ML_CBLK)
            csl = slice(cb * ML_CBLK, (cb + 1) * ML_CBLK)
            upd = _dot_tn(wkb, v_ref[:, sl])
            if recurrent:
                c_s[h, :, csl] = dec_all[h] * c_s[h, :, csl] + upd
            else:
                cout_ref[h, :, csl] = upd
        if recurrent:
            n_s[h:h + 1, :] = dec_all[h] * n_s[h:h + 1, :] + n_add
            m_s[h:h + 1, :] = m_row
        else:
            nout_ref[h:h + 1, :] = n_add
            mout_ref[h:h + 1, :] = m_row


def _lat_slot(d, g):
    return g + d * (ML_LAT_SLOTS - 1 - 2 * g)


def _mlstm_scan(proj, gates, cache_c, cache_n, cache_m, j_layer, c_acc):
    H, L = ML_HEADS, ML_CHUNK
    n_slot = T_ALL // L
    gd = jnp.stack([jnp.concatenate([gates[:, 0:H], gates[:, 2 * H:3 * H]], axis=1),
                    jnp.concatenate([gates[:, H:2 * H], gates[:, 3 * H:4 * H]], axis=1)], axis=0)
    gcol = gd.reshape(2, n_slot, L, 2 * H)
    grow = gcol.transpose(0, 1, 3, 2)
    m0 = jnp.broadcast_to(cache_m[:, j_layer][..., None], (DEC_BATCH, 2, H, LANE))
    h_shape = jax.ShapeDtypeStruct((2, T_ALL, ML_INNER), BF16)

    def token_specs(blk):
        return [
            pl.BlockSpec((L, ML_QK), lambda d, g: (blk(d, g), 0)),
            pl.BlockSpec((L, ML_QK), lambda d, g: (blk(d, g), 1)),
            pl.BlockSpec((L, ML_INNER), lambda d, g: (blk(d, g), 1)),
            pl.BlockSpec((None, None, L, 2 * H), lambda d, g: (d, blk(d, g), 0, 0)),
            pl.BlockSpec((None, None, 2 * H, L), lambda d, g: (d, blk(d, g), 0, 0)),
        ]

    ctx_specs = token_specs(lambda d, g: g)
    ctx_args = [proj, proj, proj, gcol, grow]
    aliases = {}
    if c_acc is not None:
        ctx_specs.append(pl.BlockSpec(memory_space=pl.ANY))
        ctx_args.append(c_acc)
        aliases = {len(ctx_args) - 1: 1}
    hdir, c_fin, n_fin, m_fin = pl.pallas_call(
        functools.partial(_mlstm_kernel, recurrent=False, n_alias=len(aliases)),
        grid=(2, BATCH),
        in_specs=ctx_specs,
        out_specs=[
            pl.BlockSpec((None, L, ML_INNER), lambda d, g: (d, g, 0)),
            pl.BlockSpec((None, None, None, H, ML_DK, ML_DV), lambda d, g: (g, j_layer, d, 0, 0, 0)),
            pl.BlockSpec((None, None, H, ML_DK), lambda d, g: (g, d, 0, 0)),
            pl.BlockSpec((None, None, H, LANE), lambda d, g: (g, d, 0, 0)),
        ],
        out_shape=[
            h_shape,
            jax.ShapeDtypeStruct((BATCH, N_ML, 2, H, ML_DK, ML_DV), F32),
            jax.ShapeDtypeStruct((BATCH, 2, H, ML_DK), F32),
            jax.ShapeDtypeStruct((BATCH, 2, H, LANE), F32),
        ],
        input_output_aliases=aliases,
        compiler_params=_cparams(2),
        name="mlstm_context",
    )(*ctx_args)

    lat_blk = lambda d, g: BATCH + _lat_slot(d, g)
    lat_b = lambda d, g: _lat_slot(d, g) // ML_LAT_CHUNKS
    lat_specs = token_specs(lat_blk) + [
        pl.BlockSpec((None, None, None, H, ML_DK, ML_DV), lambda d, g: (lat_b(d, g), j_layer, d, 0, 0, 0)),
        pl.BlockSpec((None, None, None, H, ML_DK), lambda d, g: (lat_b(d, g), j_layer, d, 0, 0)),
        pl.BlockSpec((None, None, H, LANE), lambda d, g: (lat_b(d, g), d, 0, 0)),
        pl.BlockSpec(memory_space=pl.ANY),
    ]
    hdir = pl.pallas_call(
        functools.partial(_mlstm_kernel, recurrent=True, n_alias=1),
        grid=(2, ML_LAT_SLOTS),
        in_specs=lat_specs,
        out_specs=pl.BlockSpec((None, L, ML_INNER), lambda d, g: (d, lat_blk(d, g), 0)),
        out_shape=h_shape,
        scratch_shapes=[
            pltpu.VMEM((H, ML_DK, ML_DV), F32),
            pltpu.VMEM((H, ML_DK), F32),
            pltpu.VMEM((H, LANE), F32),
        ],
        input_output_aliases={len(lat_specs) - 1: 0},
        compiler_params=_cparams(2),
        name="mlstm_latent",
    )(proj, proj, proj, gcol, grow, cache_c, cache_n, m0, hdir)
    return hdir, c_fin, n_fin, m_fin


def _mlstm_post_kernel(hf_ref, hb_ref, o_ref, z_ref, g_ref, y_ref):
    hs = hf_ref[...].astype(F32) + hb_ref[...].astype(F32)
    r = lax.rsqrt(jnp.mean(hs * hs, axis=-1, keepdims=True) + EPS)
    y = (hs * r * g_ref[...]) * _sigmoid(o_ref[...].astype(F32)) * _silu(z_ref[...].astype(F32))
    y_ref[...] = y.astype(BF16)


def _mlstm_post(hdir, proj, g_head):
    tm = 1024
    o_blk = (2 * ML_QK + ML_INNER) // ML_DV
    z_blk = (2 * ML_QK + 2 * ML_INNER) // ML_DV
    return pl.pallas_call(
        _mlstm_post_kernel,
        grid=(T_ALL // tm, ML_HEADS),
        in_specs=[
            pl.BlockSpec((None, tm, ML_DV), lambda i, h: (0, i, h)),
            pl.BlockSpec((None, tm, ML_DV), lambda i, h: (1, i, h)),
            pl.BlockSpec((tm, ML_DV), lambda i, h: (i, o_blk + h)),
            pl.BlockSpec((tm, ML_DV), lambda i, h: (i, z_blk + h)),
            pl.BlockSpec((1, ML_DV), lambda i, h: (0, h)),
        ],
        out_specs=pl.BlockSpec((tm, ML_DV), lambda i, h: (i, h)),
        out_shape=jax.ShapeDtypeStruct((T_ALL, ML_INNER), BF16),
        compiler_params=_cparams(2),
        name="mlstm_gate_norm",
    )(hdir, hdir, proj, proj, g_head.reshape(1, ML_INNER))


def _dwconv3_tile(x, w_ref, row_len, tm):
    r = lax.broadcasted_iota(jnp.int32, (tm, 1), 0) & (row_len - 1)
    prev = jnp.where(r == 0, 0.0, pltpu.roll(x, 1, axis=0))
    nxt = jnp.where(r == row_len - 1, 0.0, pltpu.roll(x, tm - 1, axis=0))
    return prev * w_ref[0:1, :] + x * w_ref[1:2, :] + nxt * w_ref[2:3, :]


def _sc_kernel(a_ref, wu_ref, wb_ref, wc_ref, wz_ref, cw_ref, y_ref, wbuf, *, tm):
    i = pl.program_id(1)

    @pl.when(i == 0)
    def _():
        for g, w_ref in enumerate((wu_ref, wb_ref, wc_ref, wz_ref)):
            wbuf[g] = w_ref[...].astype(BF16)

    a = a_ref[...]
    u, b, c, z = (_dot(a, wbuf[g]) for g in range(4))
    conv = _dwconv3_tile(c * u, cw_ref, _conv_row_len(i * tm), tm)
    y_ref[...] = (b * conv * _silu(z)).astype(BF16)


def _shortconv(a, w_in, layer, w_conv):
    tm, tn = 1024, 256
    k = a.shape[1]
    nb = SC_INNER // tn
    w_specs = [pl.BlockSpec((None, k, tn), functools.partial(lambda j, i, g: (layer, 0, g * nb + j), g=g))
               for g in range(4)]
    return pl.pallas_call(
        functools.partial(_sc_kernel, tm=tm),
        grid=(nb, T_ALL // tm),
        in_specs=[pl.BlockSpec((tm, k), lambda j, i: (i, 0)), *w_specs,
                  pl.BlockSpec((None, 3, tn), lambda j, i: (layer, 0, j))],
        out_specs=pl.BlockSpec((tm, tn), lambda j, i: (i, j)),
        out_shape=jax.ShapeDtypeStruct((T_ALL, SC_INNER), BF16),
        scratch_shapes=[pltpu.VMEM((4, k, tn), BF16)],
        compiler_params=_cparams(2),
        name="shortconv_mixer",
    )(a, w_in, w_in, w_in, w_in, w_conv)


def _gdn_qkv_kernel(a_ref, w_ref, cw_ref, o_ref, wb_ref, *, tm, tn):
    j = pl.program_id(0)
    i = pl.program_id(1)

    @pl.when(i == 0)
    def _():
        wb_ref[...] = w_ref[...].astype(BF16)

    y = _silu(_dwconv3_tile(_dot(a_ref[...], wb_ref[...]), cw_ref, _conv_row_len(i * tm), tm))
    q_blocks = GD_QK // tn
    is_qk = j < 2 * q_blocks
    q_scale = jnp.where(j < q_blocks, GD_DK ** -0.5, 1.0)
    for hh in range(tn // GD_DK):
        yh = y[:, hh * GD_DK:(hh + 1) * GD_DK]
        r = lax.rsqrt(jnp.sum(yh * yh, axis=-1, keepdims=True) + EPS) * q_scale
        o_ref[:, hh * GD_DK:(hh + 1) * GD_DK] = (yh * jnp.where(is_qk, r, 1.0)).astype(o_ref.dtype)


def _gdn_qkv(a, w_in, layer, w_conv):
    tm, tn = 1024, 1024
    k = a.shape[1]
    return pl.pallas_call(
        functools.partial(_gdn_qkv_kernel, tm=tm, tn=tn),
        grid=(GD_CONV_CH // tn, T_ALL // tm),
        in_specs=[
            pl.BlockSpec((tm, k), lambda j, i: (i, 0)),
            pl.BlockSpec((None, k, tn), lambda j, i: (layer, 0, j)),
            pl.BlockSpec((None, 3, tn), lambda j, i: (layer, 0, j)),
        ],
        out_specs=pl.BlockSpec((tm, tn), lambda j, i: (i, j)),
        out_shape=jax.ShapeDtypeStruct((T_ALL, GD_CONV_CH), BF16),
        scratch_shapes=[pltpu.VMEM((k, tn), BF16)],
        compiler_params=_cparams(2),
        name="gdn_qkv_projection",
    )(a, w_in, w_conv)


def _gdn_kernel(q_ref, k_ref, v_ref, gc_ref, gr_ref, s0_ref, o_ref, sout_ref, s_s):
    L = GD_CHUNK
    d = pl.program_id(0)
    j = GD_SLOTS.slot(d, pl.program_id(2))
    is_ctx, is_head, is_tail = GD_SLOTS.flags(d, j)

    @pl.when(jnp.logical_and(is_head, is_ctx))
    def _():
        s_s[...] = jnp.zeros_like(s_s)

    @pl.when(jnp.logical_and(is_head, jnp.logical_not(is_ctx)))
    def _():
        s_s[...] = s0_ref[...]

    nh = GD_GV
    G_c = gc_ref[:, 0:nh]
    beta_c = gc_ref[:, nh:2 * nh]
    G_r = gr_ref[0:nh, :]
    Gl_all = jnp.where(d == 0, G_c[L - 1:L, :], G_c[0:1, :])

    pairs = range(GD_GQ)
    lane_lo = lax.broadcasted_iota(jnp.int32, (L, 2 * L), 1) < L
    t_i = lax.broadcasted_iota(jnp.int32, (L, 2 * L), 0)
    s_i = lax.broadcasted_iota(jnp.int32, (L, 2 * L), 1) & (L - 1)
    diff = (t_i - s_i) * (1 - 2 * d)
    incl2, strict2 = diff >= 0, diff > 0

    def block_diag(x):
        zero = jnp.zeros_like(x)
        return jnp.concatenate([jnp.where(lane_lo, x, zero), jnp.where(lane_lo, zero, x)], axis=0)

    def stack_cols(a, p):
        return jnp.concatenate([a[:, 2 * p:2 * p + 1], a[:, 2 * p + 1:2 * p + 2]], axis=0)

    qb, k32, kkqk = [], [], []
    for p in pairs:
        kb = k_ref[:, p * GD_DK:(p + 1) * GD_DK]
        qb.append(q_ref[:, p * GD_DK:(p + 1) * GD_DK])
        k32.append(kb.astype(F32))
        kkqk.append(_dot_nt(jnp.concatenate([kb, qb[p]], axis=0), jnp.concatenate([kb, kb], axis=0)))

    neg_a, p_mat, rhs, eG = [], [], [], []
    for p in pairs:
        Gc2 = jnp.where(lane_lo, G_c[:, 2 * p:2 * p + 1], G_c[:, 2 * p + 1:2 * p + 2])
        Gr2 = jnp.concatenate([G_r[2 * p:2 * p + 1, :], G_r[2 * p + 1:2 * p + 2, :]], axis=1)
        beta2 = jnp.where(lane_lo, beta_c[:, 2 * p:2 * p + 1], beta_c[:, 2 * p + 1:2 * p + 2])
        decay = jnp.where(incl2, jnp.exp(jnp.where(incl2, Gc2 - Gr2, 0.0)), 0.0)
        neg_a.append(jnp.where(strict2, -(beta2 * kkqk[p][0:L] * decay), 0.0))
        p_mat.append((kkqk[p][L:2 * L] * decay).astype(BF16))
        beta_s = stack_cols(beta_c, p)
        eG.append(jnp.exp(stack_cols(G_c, p)))
        k2 = jnp.concatenate([k32[p], k32[p]], axis=0)
        v2 = v_ref[:, 2 * p * GD_DV:(2 * p + 2) * GD_DV].astype(F32)
        v2 = jnp.concatenate([v2[:, 0:GD_DV], v2[:, GD_DV:2 * GD_DV]], axis=0)
        rhs.append(jnp.concatenate([(beta_s * eG[p]) * k2, beta_s * v2], axis=-1))

    n_hi = [a.astype(BF16) for a in neg_a]
    n_bd = [block_diag(x) for x in n_hi]
    m_inv = list(neg_a)
    p32 = [_dot(x, b) for x, b in zip(n_hi, n_bd)]
    for k in range(1, 6):
        pw = [x.astype(BF16) for x in p32]
        pw_bd = [block_diag(x) for x in pw]
        if k < 5:
            both = [_dot(jnp.concatenate([x, m.astype(BF16)], axis=0), b) for x, m, b in zip(pw, m_inv, pw_bd)]
            m_inv = [m + x + b[L:2 * L] for m, x, b in zip(m_inv, p32, both)]
            p32 = [b[0:L] for b in both]
        else:
            m_inv = [m + x + _dot(m.astype(BF16), b) for m, x, b in zip(m_inv, p32, pw_bd)]
    m_bd = [block_diag(m.astype(BF16)) for m in m_inv]
    x1 = [r + _dot(m, r.astype(BF16)) for m, r in zip(m_bd, rhs)]
    res = []
    for r, x, a, ah, ah_bd in zip(rhs, x1, neg_a, n_hi, n_bd):
        al_bd = block_diag((a - ah.astype(F32)).astype(BF16))
        xh = x.astype(BF16)
        xl = (x - xh.astype(F32)).astype(BF16)
        hi = _dot(jnp.concatenate([ah_bd, al_bd], axis=0), xh)
        res.append((r - x) + (hi[0:2 * L] + hi[2 * L:4 * L] + _dot(ah_bd, xl)))
    sol = [x + r + _dot(m, r.astype(BF16)) for x, r, m in zip(x1, res, m_bd)]

    heads = range(nh)
    rows = [slice((hv % 2) * L, (hv % 2 + 1) * L) for hv in heads]
    S = [s_s[hv] for hv in heads]
    Sb = [x.astype(BF16) for x in S]
    wq = [_dot(jnp.concatenate([sol[hv // 2][rows[hv], 0:GD_DK].astype(BF16), qb[hv // 2]], axis=0), Sb[hv])
          for hv in heads]
    ub = [jnp.concatenate([sol[p][rows[2 * p + r], GD_DK:GD_DK + GD_DV] - wq[2 * p + r][0:L] for r in range(2)],
                          axis=0).astype(BF16) for p in pairs]
    pu = [_dot(block_diag(p_mat[p]), ub[p]) for p in pairs]
    kdec = [(jnp.exp(Gl_all[:, hv:hv + 1] - G_c[:, hv:hv + 1]) * k32[hv // 2]).astype(BF16) for hv in heads]
    ku = [_dot_tn(kdec[hv], ub[hv // 2][rows[hv]]) for hv in heads]
    for hv in heads:
        o = eG[hv // 2][rows[hv]] * wq[hv][L:2 * L] + pu[hv // 2][rows[hv]]
        o_ref[:, hv * GD_DV:(hv + 1) * GD_DV] = o.astype(o_ref.dtype)
        s_s[hv] = jnp.exp(Gl_all[:, hv:hv + 1]) * S[hv] + ku[hv]

    @pl.when(jnp.logical_and(is_tail, is_ctx))
    def _():
        sout_ref[...] = s_s[...]


def _gdn_scan(qkv, ab, cache_s, j_layer):
    HV, nh, L, S = GD_V_HEADS, GD_GV, GD_CHUNK, GD_SLOTS
    a_d = ab[:, 0:2 * HV].reshape(T_ALL, 2, GD_GROUPS, nh)
    b_d = ab[:, 2 * HV:4 * HV].reshape(T_ALL, 2, GD_GROUPS, nh)
    gcol = jnp.concatenate([a_d, b_d], axis=-1).reshape(S.n, L, 2, GD_GROUPS, 2 * nh)
    gcol = gcol.transpose(2, 3, 0, 1, 4)
    grow = gcol.transpose(0, 1, 2, 4, 3)

    qb = GD_GQ * GD_DK
    vb = nh * GD_DV
    return pl.pallas_call(
        _gdn_kernel,
        grid=(2, GD_GROUPS, S.n),
        in_specs=[
            pl.BlockSpec((L, qb), lambda d, p, g: (S.slot(d, g), p)),
            pl.BlockSpec((L, qb), lambda d, p, g: (S.slot(d, g), GD_GROUPS + p)),
            pl.BlockSpec((L, vb), lambda d, p, g: (S.slot(d, g), GD_GROUPS + p)),
            pl.BlockSpec((None, None, None, L, 2 * nh), lambda d, p, g: (d, p, S.slot(d, g), 0, 0)),
            pl.BlockSpec((None, None, None, 2 * nh, L), lambda d, p, g: (d, p, S.slot(d, g), 0, 0)),
            pl.BlockSpec((None, None, None, nh, GD_DK, GD_DV),
                         lambda d, p, g: (S.lat_batch(d, g), j_layer, d, p, 0, 0)),
        ],
        out_specs=[
            pl.BlockSpec((None, L, vb), lambda d, p, g: (d, S.slot(d, g), p)),
            pl.BlockSpec((None, None, nh, GD_DK, GD_DV), lambda d, p, g: (S.ctx_seq(d, g), d, p, 0, 0)),
        ],
        out_shape=[
            jax.ShapeDtypeStruct((2, T_ALL, GD_INNER), BF16),
            jax.ShapeDtypeStruct((BATCH, 2, HV, GD_DK, GD_DV), F32),
        ],
        scratch_shapes=[pltpu.VMEM((nh, GD_DK, GD_DV), F32)],
        compiler_params=_cparams(3),
        name="gdn_scan",
    )(qkv, qkv, qkv, gcol, grow, cache_s)


def _gdn_post_kernel(of_ref, ob_ref, z_ref, g_ref, y_ref, *, tn):
    g = g_ref[...]
    for hh in range(tn // GD_DV):
        sl = slice(hh * GD_DV, (hh + 1) * GD_DV)
        o = of_ref[:, sl].astype(F32) + ob_ref[:, sl].astype(F32)
        r = lax.rsqrt(jnp.mean(o * o, axis=-1, keepdims=True) + EPS)
        y_ref[:, sl] = ((o * r * g) * _silu(z_ref[:, sl].astype(F32))).astype(BF16)


def _gdn_post(odir, z, g_norm):
    tm, tn = 1024, 512
    return pl.pallas_call(
        functools.partial(_gdn_post_kernel, tn=tn),
        grid=(T_ALL // tm, GD_INNER // tn),
        in_specs=[
            pl.BlockSpec((None, tm, tn), lambda i, j: (0, i, j)),
            pl.BlockSpec((None, tm, tn), lambda i, j: (1, i, j)),
            pl.BlockSpec((tm, tn), lambda i, j: (i, j)),
            pl.BlockSpec((1, GD_DV), lambda i, j: (0, 0)),
        ],
        out_specs=pl.BlockSpec((tm, tn), lambda i, j: (i, j)),
        out_shape=jax.ShapeDtypeStruct((T_ALL, GD_INNER), BF16),
        compiler_params=_cparams(2),
        name="gdn_gate_norm",
    )(odir, odir, z, g_norm.reshape(1, GD_DV))


def kernel(x_prompt, x_sample, c, cache_ml_C, cache_ml_n, cache_ml_m, cache_gd_S, c_ctx, w_ada, b_ada, g_norm,
           w_ml_in, b_ml_gate, g_ml_head, w_ml_out, w_sc_in, w_sc_conv, w_sc_out, w_gd_in, w_gd_conv, gd_A_log,
           gd_dt_bias, g_gd_norm, w_gd_out, g_final):
    xs = (x_prompt.reshape(T_CTX, D_MODEL), x_sample.reshape(T_LAT, D_MODEL))
    cond = jnp.concatenate([c_ctx[None, :], c, jnp.zeros((N_COND - 1 - DEC_BATCH, D_MODEL), F32)], axis=0)
    mod = _modulation(cond, w_ada, b_ada)
    w_ml_nk = jnp.swapaxes(w_ml_in, 1, 2)

    state_ml_c = None
    ml_n, ml_m, gd_s = [], [], []
    for l in range(DEPTH):
        shift = mod[l, :, 0:D_MODEL]
        scale = mod[l, :, D_MODEL:2 * D_MODEL]
        gate = mod[l, :, 2 * D_MODEL:3 * D_MODEL]
        h = _norm_mod(xs, g_norm[l], shift, scale)
        j = l // 3
        kind = l % 3
        if kind == 0:
            proj = _project(h, w_ml_nk, j, 0, ML_MAIN, 1024, BF16, w_is_nk=True)
            gates = _ml_gates(h, w_ml_nk, j, b_ml_gate[j])
            hdir, state_ml_c, n_fin, m_fin = _mlstm_scan(proj, gates, cache_ml_C, cache_ml_n, cache_ml_m, j,
                                                         state_ml_c)
            ml_n.append(n_fin)
            ml_m.append(m_fin[..., 0])
            y = _mlstm_post(hdir, proj, g_ml_head[j])
            xs = (_project_residual(y, w_ml_out, j, xs, gate),)
        elif kind == 1:
            y = _shortconv(h, w_sc_in, j, w_sc_conv)
            xs = (_project_residual(y, w_sc_out, j, xs, gate),)
        else:
            qkv = _gdn_qkv(h, w_gd_in, j, w_gd_conv)
            z = _project(h, w_gd_in, j, GD_CONV_CH, GD_INNER, 1024, BF16)
            ab = _gd_gates(h, w_gd_in, j, gd_A_log[j], gd_dt_bias[j])
            odir, s_fin = _gdn_scan(qkv, ab, cache_gd_S, j)
            gd_s.append(s_fin)
            y = _gdn_post(odir, z, g_gd_norm[j])
            xs = (_project_residual(y, w_gd_out, j, xs, gate),)

    y_ctx, y_lat = _final_norm(xs[0], g_final)
    y_prompt = y_ctx.reshape(BATCH, SEQ, D_MODEL)
    y_sample = y_lat.reshape(DEC_BATCH, DEC_SEQ, D_MODEL)
    state_ml_n = jnp.stack(ml_n, axis=1)
    state_ml_m = jnp.stack(ml_m, axis=1)
    state_gd_s = jnp.stack(gd_s, axis=1)
    return (y_prompt, y_sample, state_ml_c, state_ml_n, state_ml_m, state_gd_s)
```

```python
import functools

import jax
import jax.numpy as jnp
from jax import lax
from jax.experimental import pallas as pl
from jax.experimental.pallas import tpu as pltpu

F32 = jnp.float32
BF16 = jnp.bfloat16
HIGHEST = lax.Precision.HIGHEST

D_MODEL = 2048
BATCH = 16
SEQ = 256
DEPTH = 4
DEC_BATCH = 2
DEC_SEQ = 2048
GRID_W = 64
EPS = 1e-6

T_CTX = BATCH * SEQ
T_LAT = DEC_BATCH * DEC_SEQ
T_ALL = T_CTX + T_LAT
N_COND = 8

ML_HEADS = 8
ML_DK = 256
ML_DV = 512
ML_QK = ML_HEADS * ML_DK
ML_INNER = ML_HEADS * ML_DV
ML_MAIN = 2 * ML_QK + 3 * ML_INNER
N_ML = 2
ML_CHUNK = SEQ
ML_LAT_CHUNKS = DEC_SEQ // ML_CHUNK
ML_LAT_SLOTS = DEC_BATCH * ML_LAT_CHUNKS
ML_CBLK = 256

SC_INNER = 2 * D_MODEL

GD_DK = 128
GD_DV = 128
GD_QK_HEADS = 16
GD_V_HEADS = 32
GD_QK = GD_QK_HEADS * GD_DK
GD_INNER = GD_V_HEADS * GD_DV
GD_CONV_CH = 2 * GD_QK + GD_INNER
GD_MAIN = GD_CONV_CH + GD_INNER
GD_GROUPS = 1
GD_GV = GD_V_HEADS // GD_GROUPS
GD_GQ = GD_QK_HEADS // GD_GROUPS
GD_CHUNK = 64

LANE = 128
BF16_ROWS = 16
VMEM_LIMIT = 56 * 1024 * 1024


def _cparams(n_axes):
    return pltpu.CompilerParams(dimension_semantics=("arbitrary",) * n_axes, vmem_limit_bytes=VMEM_LIMIT)


def _sigmoid(x):
    return 0.5 * jnp.tanh(0.5 * x) + 0.5


def _silu(x):
    h = 0.5 * x
    return h + h * jnp.tanh(h)


def _softplus(x):
    return jnp.maximum(x, 0.0) + jnp.log1p(jnp.exp(-jnp.abs(x)))


def _log_sigmoid(x):
    return -_softplus(-x)


def _cond_group(row0):
    return jnp.where(row0 < T_CTX, 0, 1 + (row0 - T_CTX) // DEC_SEQ)


def _conv_row_len(row0):
    return jnp.where(row0 < T_CTX, SEQ, GRID_W)


def _dot_nt(a, b, **kw):
    return lax.dot_general(a, b, (((1,), (1,)), ((), ())), preferred_element_type=F32, **kw)


def _dot_tn(a, b, **kw):
    return lax.dot_general(a, b, (((0,), (0,)), ((), ())), preferred_element_type=F32, **kw)


def _dot(a, b, **kw):
    return jnp.dot(a, b, preferred_element_type=F32, **kw)


def _mod_kernel(s_ref, w_ref, b_ref, o_ref):
    a = _silu(s_ref[...]).astype(BF16)
    o_ref[...] = _dot(a, w_ref[...].astype(BF16)) + b_ref[...]


def _modulation(cond, w_ada, b_ada):
    tn = 1024
    n = 3 * D_MODEL
    return pl.pallas_call(
        _mod_kernel,
        grid=(DEPTH, n // tn),
        in_specs=[
            pl.BlockSpec((N_COND, D_MODEL), lambda l, j: (0, 0)),
            pl.BlockSpec((None, D_MODEL, tn), lambda l, j: (l, 0, j)),
            pl.BlockSpec((None, 1, tn), lambda l, j: (l, 0, j)),
        ],
        out_specs=pl.BlockSpec((None, N_COND, tn), lambda l, j: (l, 0, j)),
        out_shape=jax.ShapeDtypeStruct((DEPTH, N_COND, n), F32),
        compiler_params=_cparams(2),
        name="adaln_modulation",
    )(cond, w_ada, b_ada.reshape(DEPTH, 1, n))


def _split_specs(block, index_map, tm):
    n_ctx = T_CTX // tm

    def ctx_map(*idx):
        row, *rest = index_map(*idx)
        return (jnp.minimum(row, n_ctx - 1), *rest)

    def lat_map(*idx):
        row, *rest = index_map(*idx)
        return (jnp.maximum(row - n_ctx, 0), *rest)

    return [pl.BlockSpec(block, ctx_map), pl.BlockSpec(block, lat_map)]


def _token_tile(refs, row0):
    if len(refs) == 1:
        return refs[0][...]
    return jnp.where(row0 < T_CTX, refs[0][...], refs[1][...])


def _norm_mod_kernel(*refs, tm):
    *x_refs, g_ref, shift_ref, scale_ref, o_ref = refs
    row0 = pl.program_id(0) * tm
    grp = _cond_group(row0)
    x = _token_tile(x_refs, row0)
    r = lax.rsqrt(jnp.mean(x * x, axis=-1, keepdims=True) + EPS)
    xn = x * r * g_ref[...]
    sh = shift_ref[pl.ds(grp, 1), :]
    sc = scale_ref[pl.ds(grp, 1), :]
    o_ref[...] = (xn * (1.0 + sc) + sh).astype(BF16)


def _norm_mod(xs, g, shift, scale):
    tm = 512
    x_map = lambda i: (i, 0)
    x_specs = [pl.BlockSpec((tm, D_MODEL), x_map)] if len(xs) == 1 else _split_specs((tm, D_MODEL), x_map, tm)
    return pl.pallas_call(
        functools.partial(_norm_mod_kernel, tm=tm),
        grid=(T_ALL // tm,),
        in_specs=[
            *x_specs,
            pl.BlockSpec((1, D_MODEL), lambda i: (0, 0)),
            pl.BlockSpec((N_COND, D_MODEL), lambda i: (0, 0)),
            pl.BlockSpec((N_COND, D_MODEL), lambda i: (0, 0)),
        ],
        out_specs=pl.BlockSpec((tm, D_MODEL), lambda i: (i, 0)),
        out_shape=jax.ShapeDtypeStruct((T_ALL, D_MODEL), BF16),
        compiler_params=_cparams(1),
        name="norm_modulate",
    )(*xs, g.reshape(1, D_MODEL), shift, scale)


def _final_norm_kernel(x_ref, g_ref, ctx_ref, lat_ref, *, ctx_tiles):
    x = x_ref[...]
    r = lax.rsqrt(jnp.mean(x * x, axis=-1, keepdims=True) + EPS)
    y = x * r * g_ref[...]
    i = pl.program_id(0)

    @pl.when(i < ctx_tiles)
    def _():
        ctx_ref[...] = y

    @pl.when(i >= ctx_tiles)
    def _():
        lat_ref[...] = y


def _final_norm(x, g):
    tm = 512
    ctx_tiles = T_CTX // tm
    return pl.pallas_call(
        functools.partial(_final_norm_kernel, ctx_tiles=ctx_tiles),
        grid=(T_ALL // tm,),
        in_specs=[
            pl.BlockSpec((tm, D_MODEL), lambda i: (i, 0)),
            pl.BlockSpec((1, D_MODEL), lambda i: (0, 0)),
        ],
        out_specs=[
            pl.BlockSpec((tm, D_MODEL), lambda i: (jnp.minimum(i, ctx_tiles - 1), 0)),
            pl.BlockSpec((tm, D_MODEL), lambda i: (jnp.maximum(i - ctx_tiles, 0), 0)),
        ],
        out_shape=[
            jax.ShapeDtypeStruct((T_CTX, D_MODEL), F32),
            jax.ShapeDtypeStruct((T_LAT, D_MODEL), F32),
        ],
        compiler_params=_cparams(1),
        name="final_norm",
    )(x, g.reshape(1, D_MODEL))


def _proj_kernel(a_ref, w_ref, o_ref, wb_ref, *, w_is_nk):
    @pl.when(pl.program_id(1) == 0)
    def _():
        wb_ref[...] = w_ref[...].astype(BF16)

    dot = _dot_nt if w_is_nk else _dot
    o_ref[...] = dot(a_ref[...], wb_ref[...]).astype(o_ref.dtype)


def _project(a, w, layer, col0, n_cols, tn, out_dtype, w_is_nk=False, tm=2048):
    k = a.shape[1]
    j0 = col0 // tn
    if w_is_nk:
        w_spec = pl.BlockSpec((None, tn, k), lambda j, i: (layer, j0 + j, 0))
        w_tile = (tn, k)
    else:
        w_spec = pl.BlockSpec((None, k, tn), lambda j, i: (layer, 0, j0 + j))
        w_tile = (k, tn)
    return pl.pallas_call(
        functools.partial(_proj_kernel, w_is_nk=w_is_nk),
        grid=(n_cols // tn, T_ALL // tm),
        in_specs=[pl.BlockSpec((tm, k), lambda j, i: (i, 0)), w_spec],
        out_specs=pl.BlockSpec((tm, tn), lambda j, i: (i, j)),
        out_shape=jax.ShapeDtypeStruct((T_ALL, n_cols), out_dtype),
        scratch_shapes=[pltpu.VMEM(w_tile, BF16)],
        compiler_params=_cparams(2),
        name="in_projection",
    )(a, w)


def _proj_res_kernel(a_ref, w_ref, *refs, tm):
    *x_refs, gate_ref, o_ref, wb_ref = refs
    i = pl.program_id(1)

    @pl.when(i == 0)
    def _():
        wb_ref[...] = w_ref[...].astype(BF16)

    gate = gate_ref[pl.ds(_cond_group(i * tm), 1), :]
    o_ref[...] = _token_tile(x_refs, i * tm) + gate * _dot(a_ref[...], wb_ref[...])


def _project_residual(a, w, layer, xs, gate):
    k = a.shape[1]
    tm, tn = 1024, 512
    x_map = lambda j, i: (i, j)
    x_specs = [pl.BlockSpec((tm, tn), x_map)] if len(xs) == 1 else _split_specs((tm, tn), x_map, tm)
    return pl.pallas_call(
        functools.partial(_proj_res_kernel, tm=tm),
        grid=(D_MODEL // tn, T_ALL // tm),
        in_specs=[
            pl.BlockSpec((tm, k), lambda j, i: (i, 0)),
            pl.BlockSpec((None, k, tn), lambda j, i: (layer, 0, j)),
            *x_specs,
            pl.BlockSpec((N_COND, tn), lambda j, i: (0, j)),
        ],
        out_specs=pl.BlockSpec((tm, tn), lambda j, i: (i, j)),
        out_shape=jax.ShapeDtypeStruct((T_ALL, D_MODEL), F32),
        scratch_shapes=[pltpu.VMEM((k, tn), BF16)],
        compiler_params=_cparams(2),
        name="out_projection_residual",
    )(a, w, *xs, gate)


def _chunk_cumsums(x_fwd, x_bwd, chunk):
    r = lax.broadcasted_iota(jnp.int32, (chunk, chunk), 0)
    c = lax.broadcasted_iota(jnp.int32, (chunk, chunk), 1)
    lower = (c <= r).astype(F32)
    upper = (c >= r).astype(F32)
    rows = [slice(i * chunk, (i + 1) * chunk) for i in range(x_fwd.shape[0] // chunk)]
    fwd = jnp.concatenate([_dot(lower, x_fwd[sl], precision=HIGHEST) for sl in rows], axis=0)
    bwd = jnp.concatenate([_dot(upper, x_bwd[sl], precision=HIGHEST) for sl in rows], axis=0)
    return fwd, bwd


def _ml_gate_kernel(a_ref, w_ref, b_ref, o_ref):
    H = ML_HEADS
    x = _dot_nt(a_ref[...], w_ref[...].astype(BF16)) + b_ref[...]
    lf = _log_sigmoid(x[:, 2 * H:4 * H])
    o_ref[:, 0:2 * H] = x[:, 0:2 * H]
    o_ref[:, 2 * H:3 * H], o_ref[:, 3 * H:4 * H] = _chunk_cumsums(lf[:, 0:H], lf[:, H:2 * H], ML_CHUNK)


def _ml_gates(a, w_nk, layer, b_gate):
    tm, n = 512, 4 * ML_HEADS
    k = a.shape[1]
    return pl.pallas_call(
        _ml_gate_kernel,
        grid=(T_ALL // tm,),
        in_specs=[
            pl.BlockSpec((tm, k), lambda i: (i, 0)),
            pl.BlockSpec((None, n, k), lambda i: (layer, ML_MAIN // n, 0)),
            pl.BlockSpec((1, n), lambda i: (0, 0)),
        ],
        out_specs=pl.BlockSpec((tm, n), lambda i: (i, 0)),
        out_shape=jax.ShapeDtypeStruct((T_ALL, n), F32),
        compiler_params=_cparams(1),
        name="mlstm_gates",
    )(a, w_nk, b_gate.reshape(1, n))


def _gd_gate_kernel(a_ref, w_ref, p_ref, o_ref):
    HV = GD_V_HEADS
    x = _dot(a_ref[...], w_ref[...].astype(BF16))
    g = -jnp.exp(p_ref[0:1, :]) * _softplus(x[:, 0:2 * HV] + p_ref[1:2, :])
    o_ref[:, 0:HV], o_ref[:, HV:2 * HV] = _chunk_cumsums(g[:, 0:HV], g[:, HV:2 * HV], GD_CHUNK)
    o_ref[:, 2 * HV:4 * HV] = _sigmoid(x[:, 2 * HV:4 * HV])


def _gd_gates(a, w_in, layer, a_log, dt_bias):
    tm, n = 512, 4 * GD_V_HEADS
    k = a.shape[1]
    par = jnp.stack([a_log.astype(F32).reshape(-1), dt_bias.astype(F32).reshape(-1)], axis=0)
    return pl.pallas_call(
        _gd_gate_kernel,
        grid=(T_ALL // tm,),
        in_specs=[
            pl.BlockSpec((tm, k), lambda i: (i, 0)),
            pl.BlockSpec((None, k, n), lambda i: (layer, 0, GD_MAIN // n)),
            pl.BlockSpec((2, n // 2), lambda i: (0, 0)),
        ],
        out_specs=pl.BlockSpec((tm, n), lambda i: (i, 0)),
        out_shape=jax.ShapeDtypeStruct((T_ALL, n), F32),
        compiler_params=_cparams(1),
        name="gdn_gates",
    )(a, w_in, par)


class _Slots:
    def __init__(self, chunk):
        self.chunk = chunk
        self.n = T_ALL // chunk
        self.ctx = T_CTX // chunk
        self.ctx_chunks = SEQ // chunk
        self.lat_chunks = DEC_SEQ // chunk

    def slot(self, d, g):
        return g + d * (self.n - 1 - 2 * g)

    def seq(self, j):
        return jnp.where(j < self.ctx, j // self.ctx_chunks, BATCH + (j - self.ctx) // self.lat_chunks)

    def flags(self, d, j):
        is_ctx = j < self.ctx
        pos = jnp.where(is_ctx, j % self.ctx_chunks, (j - self.ctx) % self.lat_chunks)
        n_chunks = jnp.where(is_ctx, self.ctx_chunks, self.lat_chunks)
        head_pos = jnp.where(d == 0, 0, n_chunks - 1)
        tail_pos = jnp.where(d == 0, n_chunks - 1, 0)
        return is_ctx, pos == head_pos, pos == tail_pos

    def mask(self, d):
        t_i = lax.broadcasted_iota(jnp.int32, (self.chunk, self.chunk), 0)
        s_i = lax.broadcasted_iota(jnp.int32, (self.chunk, self.chunk), 1)
        diff = (t_i - s_i) * (1 - 2 * d)
        return diff >= 0, diff > 0

    def lat_batch(self, d, g):
        return jnp.clip(self.seq(self.slot(d, g)) - BATCH, 0, DEC_BATCH - 1)

    def ctx_seq(self, d, g):
        return jnp.minimum(self.seq(self.slot(d, g)), BATCH - 1)


GD_SLOTS = _Slots(GD_CHUNK)


def _mlstm_kernel(*refs, recurrent, n_alias):
    q_ref, k_ref, v_ref, gc_ref, gr_ref, *rest = refs
    if recurrent:
        c0_ref, n0_ref, m0_ref, *rest = rest
    rest = rest[n_alias:]
    if recurrent:
        h_ref, c_s, n_s, m_s = rest
    else:
        h_ref, cout_ref, nout_ref, mout_ref = rest
    L = ML_CHUNK
    d = pl.program_id(0)

    if recurrent:
        pos = _lat_slot(d, pl.program_id(1)) % ML_LAT_CHUNKS

        @pl.when(pos == jnp.where(d == 0, 0, ML_LAT_CHUNKS - 1))
        def _():
            c_s[...] = c0_ref[...]
            n_s[...] = n0_ref[...]
            m_s[...] = m0_ref[...]

    t_i = lax.broadcasted_iota(jnp.int32, (L, L), 0)
    s_i = lax.broadcasted_iota(jnp.int32, (L, L), 1)
    mask = (t_i - s_i) * (1 - 2 * d) >= 0
    gc = gc_ref[...]
    gr = gr_ref[...]
    i_c = gc[:, 0:ML_HEADS]
    b_c = gc[:, ML_HEADS:2 * ML_HEADS]
    i_r = gr[0:ML_HEADS, :]
    b_r = gr[ML_HEADS:2 * ML_HEADS, :]
    bl_all = jnp.where(d == 0, b_c[L - 1:L, :], b_c[0:1, :])

    qbs = [q_ref[:, h * ML_DK:(h + 1) * ML_DK] * (ML_DK ** -0.5) for h in range(ML_HEADS)]
    a_all, m_new_all, dec_all, z_all, mt_all, sc_all = [], [], [], [], [], []
    for h in range(ML_HEADS):
        bcol = b_c[:, h:h + 1]
        m = m_s[h:h + 1, 0:1] if recurrent else 0.0
        z = jnp.where(mask, i_r[h:h + 1, :] - b_r[h:h + 1, :], -jnp.inf)
        inter = bcol + m
        mt = jnp.maximum(inter, bcol + jnp.max(z, axis=-1, keepdims=True))
        z_all.append(bcol - mt)
        mt_all.append(mt)
        bl = bl_all[:, h:h + 1]
        a = bl - bcol + i_c[:, h:h + 1]
        m_new = jnp.maximum(bl + m, jnp.max(a, axis=0, keepdims=True))
        a_all.append(a)
        m_new_all.append(m_new)
        if recurrent:
            sc_all.append(jnp.exp(inter - mt))
            dec_all.append(jnp.exp(bl + m - m_new))
    qkn = []
    for h in range(ML_HEADS):
        keys = k_ref[:, h * ML_DK:(h + 1) * ML_DK]
        if recurrent:
            nb = jnp.broadcast_to(n_s[h:h + 1, :], (BF16_ROWS, ML_DK)).astype(BF16)
            keys = jnp.concatenate([keys, nb], axis=0)
        qkn.append(_dot_nt(qbs[h], keys))
    lhs, rinv = [], []
    for h in range(ML_HEADS):
        z = jnp.where(mask, i_r[h:h + 1, :] - b_r[h:h + 1, :], -jnp.inf)
        s = qkn[h][:, 0:L] * jnp.exp(z + z_all[h])
        den = jnp.sum(s, axis=-1, keepdims=True)
        if recurrent:
            den = sc_all[h] * qkn[h][:, L:L + 1] + den
            lhs.append(jnp.concatenate([(qbs[h].astype(F32) * sc_all[h]).astype(BF16), s.astype(BF16)], axis=1))
        else:
            lhs.append(s.astype(BF16))
        rinv.append(1.0 / jnp.maximum(jnp.abs(den), jnp.exp(-mt_all[h])))
    for h in range(ML_HEADS):
        w = v_ref[:, h * ML_DV:(h + 1) * ML_DV]
        if recurrent:
            w = jnp.concatenate([c_s[h].astype(BF16), w], axis=0)
        h_ref[:, h * ML_DV:(h + 1) * ML_DV] = (_dot(lhs[h], w) * rinv[h]).astype(h_ref.dtype)

    for h in range(ML_HEADS):
        wk = jnp.exp(a_all[h] - m_new_all[h]) * k_ref[:, h * ML_DK:(h + 1) * ML_DK].astype(F32)
        wkb = wk.astype(BF16)
        n_add = jnp.sum(wk, axis=0, keepdims=True)
        m_row = jnp.broadcast_to(m_new_all[h], (1, LANE))
        for cb in range(ML_DV // ML_CBLK):
            sl = slice(h * ML_DV + cb * ML_CBLK, h * ML_DV + (cb + 1) * ML_CBLK)
            csl = slice(cb * ML_CBLK, (cb + 1) * ML_CBLK)
            upd = _dot_tn(wkb, v_ref[:, sl])
            if recurrent:
                c_s[h, :, csl] = dec_all[h] * c_s[h, :, csl] + upd
            else:
                cout_ref[h, :, csl] = upd
        if recurrent:
            n_s[h:h + 1, :] = dec_all[h] * n_s[h:h + 1, :] + n_add
            m_s[h:h + 1, :] = m_row
        else:
            nout_ref[h:h + 1, :] = n_add
            mout_ref[h:h + 1, :] = m_row


def _lat_slot(d, g):
    return g + d * (ML_LAT_SLOTS - 1 - 2 * g)


def _mlstm_scan(proj, gates, cache_c, cache_n, cache_m, j_layer, c_acc):
    H, L = ML_HEADS, ML_CHUNK
    n_slot = T_ALL // L
    gd = jnp.stack([jnp.concatenate([gates[:, 0:H], gates[:, 2 * H:3 * H]], axis=1),
                    jnp.concatenate([gates[:, H:2 * H], gates[:, 3 * H:4 * H]], axis=1)], axis=0)
    gcol = gd.reshape(2, n_slot, L, 2 * H)
    grow = gcol.transpose(0, 1, 3, 2)
    m0 = jnp.broadcast_to(cache_m[:, j_layer][..., None], (DEC_BATCH, 2, H, LANE))
    h_shape = jax.ShapeDtypeStruct((2, T_ALL, ML_INNER), BF16)

    def token_specs(blk):
        return [
            pl.BlockSpec((L, ML_QK), lambda d, g: (blk(d, g), 0)),
            pl.BlockSpec((L, ML_QK), lambda d, g: (blk(d, g), 1)),
            pl.BlockSpec((L, ML_INNER), lambda d, g: (blk(d, g), 1)),
            pl.BlockSpec((None, None, L, 2 * H), lambda d, g: (d, blk(d, g), 0, 0)),
            pl.BlockSpec((None, None, 2 * H, L), lambda d, g: (d, blk(d, g), 0, 0)),
        ]

    ctx_specs = token_specs(lambda d, g: g)
    ctx_args = [proj, proj, proj, gcol, grow]
    aliases = {}
    if c_acc is not None:
        ctx_specs.append(pl.BlockSpec(memory_space=pl.ANY))
        ctx_args.append(c_acc)
        aliases = {len(ctx_args) - 1: 1}
    hdir, c_fin, n_fin, m_fin = pl.pallas_call(
        functools.partial(_mlstm_kernel, recurrent=False, n_alias=len(aliases)),
        grid=(2, BATCH),
        in_specs=ctx_specs,
        out_specs=[
            pl.BlockSpec((None, L, ML_INNER), lambda d, g: (d, g, 0)),
            pl.BlockSpec((None, None, None, H, ML_DK, ML_DV), lambda d, g: (g, j_layer, d, 0, 0, 0)),
            pl.BlockSpec((None, None, H, ML_DK), lambda d, g: (g, d, 0, 0)),
            pl.BlockSpec((None, None, H, LANE), lambda d, g: (g, d, 0, 0)),
        ],
        out_shape=[
            h_shape,
            jax.ShapeDtypeStruct((BATCH, N_ML, 2, H, ML_DK, ML_DV), F32),
            jax.ShapeDtypeStruct((BATCH, 2, H, ML_DK), F32),
            jax.ShapeDtypeStruct((BATCH, 2, H, LANE), F32),
        ],
        input_output_aliases=aliases,
        compiler_params=_cparams(2),
        name="mlstm_context",
    )(*ctx_args)

    lat_blk = lambda d, g: BATCH + _lat_slot(d, g)
    lat_b = lambda d, g: _lat_slot(d, g) // ML_LAT_CHUNKS
    lat_specs = token_specs(lat_blk) + [
        pl.BlockSpec((None, None, None, H, ML_DK, ML_DV), lambda d, g: (lat_b(d, g), j_layer, d, 0, 0, 0)),
        pl.BlockSpec((None, None, None, H, ML_DK), lambda d, g: (lat_b(d, g), j_layer, d, 0, 0)),
        pl.BlockSpec((None, None, H, LANE), lambda d, g: (lat_b(d, g), d, 0, 0)),
        pl.BlockSpec(memory_space=pl.ANY),
    ]
    hdir = pl.pallas_call(
        functools.partial(_mlstm_kernel, recurrent=True, n_alias=1),
        grid=(2, ML_LAT_SLOTS),
        in_specs=lat_specs,
        out_specs=pl.BlockSpec((None, L, ML_INNER), lambda d, g: (d, lat_blk(d, g), 0)),
        out_shape=h_shape,
        scratch_shapes=[
            pltpu.VMEM((H, ML_DK, ML_DV), F32),
            pltpu.VMEM((H, ML_DK), F32),
            pltpu.VMEM((H, LANE), F32),
        ],
        input_output_aliases={len(lat_specs) - 1: 0},
        compiler_params=_cparams(2),
        name="mlstm_latent",
    )(proj, proj, proj, gcol, grow, cache_c, cache_n, m0, hdir)
    return hdir, c_fin, n_fin, m_fin


def _mlstm_post_kernel(hf_ref, hb_ref, o_ref, z_ref, g_ref, y_ref):
    hs = hf_ref[...].astype(F32) + hb_ref[...].astype(F32)
    r = lax.rsqrt(jnp.mean(hs * hs, axis=-1, keepdims=True) + EPS)
    y = (hs * r * g_ref[...]) * _sigmoid(o_ref[...].astype(F32)) * _silu(z_ref[...].astype(F32))
    y_ref[...] = y.astype(BF16)


def _mlstm_post(hdir, proj, g_head):
    tm = 1024
    o_blk = (2 * ML_QK + ML_INNER) // ML_DV
    z_blk = (2 * ML_QK + 2 * ML_INNER) // ML_DV
    return pl.pallas_call(
        _mlstm_post_kernel,
        grid=(T_ALL // tm, ML_HEADS),
        in_specs=[
            pl.BlockSpec((None, tm, ML_DV), lambda i, h: (0, i, h)),
            pl.BlockSpec((None, tm, ML_DV), lambda i, h: (1, i, h)),
            pl.BlockSpec((tm, ML_DV), lambda i, h: (i, o_blk + h)),
            pl.BlockSpec((tm, ML_DV), lambda i, h: (i, z_blk + h)),
            pl.BlockSpec((1, ML_DV), lambda i, h: (0, h)),
        ],
        out_specs=pl.BlockSpec((tm, ML_DV), lambda i, h: (i, h)),
        out_shape=jax.ShapeDtypeStruct((T_ALL, ML_INNER), BF16),
        compiler_params=_cparams(2),
        name="mlstm_gate_norm",
    )(hdir, hdir, proj, proj, g_head.reshape(1, ML_INNER))


def _dwconv3_tile(x, w_ref, row_len, tm):
    r = lax.broadcasted_iota(jnp.int32, (tm, 1), 0) & (row_len - 1)
    prev = jnp.where(r == 0, 0.0, pltpu.roll(x, 1, axis=0))
    nxt = jnp.where(r == row_len - 1, 0.0, pltpu.roll(x, tm - 1, axis=0))
    return prev * w_ref[0:1, :] + x * w_ref[1:2, :] + nxt * w_ref[2:3, :]


def _sc_kernel(a_ref, wu_ref, wb_ref, wc_ref, wz_ref, cw_ref, y_ref, wbuf, *, tm):
    i = pl.program_id(1)

    @pl.when(i == 0)
    def _():
        for g, w_ref in enumerate((wu_ref, wb_ref, wc_ref, wz_ref)):
            wbuf[g] = w_ref[...].astype(BF16)

    a = a_ref[...]
    u, b, c, z = (_dot(a, wbuf[g]) for g in range(4))
    conv = _dwconv3_tile(c * u, cw_ref, _conv_row_len(i * tm), tm)
    y_ref[...] = (b * conv * _silu(z)).astype(BF16)


def _shortconv(a, w_in, layer, w_conv):
    tm, tn = 2048, 256
    k = a.shape[1]
    nb = SC_INNER // tn
    w_specs = [pl.BlockSpec((None, k, tn), functools.partial(lambda j, i, g: (layer, 0, g * nb + j), g=g))
               for g in range(4)]
    return pl.pallas_call(
        functools.partial(_sc_kernel, tm=tm),
        grid=(nb, T_ALL // tm),
        in_specs=[pl.BlockSpec((tm, k), lambda j, i: (i, 0)), *w_specs,
                  pl.BlockSpec((None, 3, tn), lambda j, i: (layer, 0, j))],
        out_specs=pl.BlockSpec((tm, tn), lambda j, i: (i, j)),
        out_shape=jax.ShapeDtypeStruct((T_ALL, SC_INNER), BF16),
        scratch_shapes=[pltpu.VMEM((4, k, tn), BF16)],
        compiler_params=_cparams(2),
        name="shortconv_mixer",
    )(a, w_in, w_in, w_in, w_in, w_conv)


def _gdn_qkv_kernel(a_ref, w_ref, cw_ref, o_ref, wb_ref, *, tm, tn):
    j = pl.program_id(0)
    i = pl.program_id(1)

    @pl.when(i == 0)
    def _():
        wb_ref[...] = w_ref[...].astype(BF16)

    y = _silu(_dwconv3_tile(_dot(a_ref[...], wb_ref[...]), cw_ref, _conv_row_len(i * tm), tm))
    q_blocks = GD_QK // tn
    is_qk = j < 2 * q_blocks
    q_scale = jnp.where(j < q_blocks, GD_DK ** -0.5, 1.0)
    for hh in range(tn // GD_DK):
        yh = y[:, hh * GD_DK:(hh + 1) * GD_DK]
        r = lax.rsqrt(jnp.sum(yh * yh, axis=-1, keepdims=True) + EPS) * q_scale
        o_ref[:, hh * GD_DK:(hh + 1) * GD_DK] = (yh * jnp.where(is_qk, r, 1.0)).astype(o_ref.dtype)


def _gdn_qkv(a, w_in, layer, w_conv):
    tm, tn = 1024, 1024
    k = a.shape[1]
    return pl.pallas_call(
        functools.partial(_gdn_qkv_kernel, tm=tm, tn=tn),
        grid=(GD_CONV_CH // tn, T_ALL // tm),
        in_specs=[
            pl.BlockSpec((tm, k), lambda j, i: (i, 0)),
            pl.BlockSpec((None, k, tn), lambda j, i: (layer, 0, j)),
            pl.BlockSpec((None, 3, tn), lambda j, i: (layer, 0, j)),
        ],
        out_specs=pl.BlockSpec((tm, tn), lambda j, i: (i, j)),
        out_shape=jax.ShapeDtypeStruct((T_ALL, GD_CONV_CH), BF16),
        scratch_shapes=[pltpu.VMEM((k, tn), BF16)],
        compiler_params=_cparams(2),
        name="gdn_qkv_projection",
    )(a, w_in, w_conv)


def _gdn_kernel(q_ref, k_ref, v_ref, gc_ref, gr_ref, s0_ref, o_ref, sout_ref, s_s):
    L = GD_CHUNK
    d = pl.program_id(0)
    j = GD_SLOTS.slot(d, pl.program_id(2))
    is_ctx, is_head, is_tail = GD_SLOTS.flags(d, j)

    @pl.when(jnp.logical_and(is_head, is_ctx))
    def _():
        s_s[...] = jnp.zeros_like(s_s)

    @pl.when(jnp.logical_and(is_head, jnp.logical_not(is_ctx)))
    def _():
        s_s[...] = s0_ref[...]

    nh = GD_GV
    G_c = gc_ref[:, 0:nh]
    beta_c = gc_ref[:, nh:2 * nh]
    G_r = gr_ref[0:nh, :]
    Gl_all = jnp.where(d == 0, G_c[L - 1:L, :], G_c[0:1, :])

    pairs = range(GD_GQ)
    lane_lo = lax.broadcasted_iota(jnp.int32, (L, 2 * L), 1) < L
    t_i = lax.broadcasted_iota(jnp.int32, (L, 2 * L), 0)
    s_i = lax.broadcasted_iota(jnp.int32, (L, 2 * L), 1) & (L - 1)
    diff = (t_i - s_i) * (1 - 2 * d)
    incl2, strict2 = diff >= 0, diff > 0

    def block_diag(x):
        zero = jnp.zeros_like(x)
        return jnp.concatenate([jnp.where(lane_lo, x, zero), jnp.where(lane_lo, zero, x)], axis=0)

    def stack_cols(a, p):
        return jnp.concatenate([a[:, 2 * p:2 * p + 1], a[:, 2 * p + 1:2 * p + 2]], axis=0)

    qb, k32, kkqk = [], [], []
    for p in pairs:
        kb = k_ref[:, p * GD_DK:(p + 1) * GD_DK]
        qb.append(q_ref[:, p * GD_DK:(p + 1) * GD_DK])
        k32.append(kb.astype(F32))
        kkqk.append(_dot_nt(jnp.concatenate([kb, qb[p]], axis=0), jnp.concatenate([kb, kb], axis=0)))

    neg_a, p_mat, rhs, eG = [], [], [], []
    for p in pairs:
        Gc2 = jnp.where(lane_lo, G_c[:, 2 * p:2 * p + 1], G_c[:, 2 * p + 1:2 * p + 2])
        Gr2 = jnp.concatenate([G_r[2 * p:2 * p + 1, :], G_r[2 * p + 1:2 * p + 2, :]], axis=1)
        beta2 = jnp.where(lane_lo, beta_c[:, 2 * p:2 * p + 1], beta_c[:, 2 * p + 1:2 * p + 2])
        decay = jnp.where(incl2, jnp.exp(jnp.where(incl2, Gc2 - Gr2, 0.0)), 0.0)
        neg_a.append(jnp.where(strict2, -(beta2 * kkqk[p][0:L] * decay), 0.0))
        p_mat.append((kkqk[p][L:2 * L] * decay).astype(BF16))
        beta_s = stack_cols(beta_c, p)
        eG.append(jnp.exp(stack_cols(G_c, p)))
        k2 = jnp.concatenate([k32[p], k32[p]], axis=0)
        v2 = v_ref[:, 2 * p * GD_DV:(2 * p + 2) * GD_DV].astype(F32)
        v2 = jnp.concatenate([v2[:, 0:GD_DV], v2[:, GD_DV:2 * GD_DV]], axis=0)
        rhs.append(jnp.concatenate([(beta_s * eG[p]) * k2, beta_s * v2], axis=-1))

    n_hi = [a.astype(BF16) for a in neg_a]
    n_bd = [block_diag(x) for x in n_hi]
    m_inv = list(neg_a)
    p32 = [_dot(x, b) for x, b in zip(n_hi, n_bd)]
    for k in range(1, 6):
        pw = [x.astype(BF16) for x in p32]
        pw_bd = [block_diag(x) for x in pw]
        if k < 5:
            both = [_dot(jnp.concatenate([x, m.astype(BF16)], axis=0), b) for x, m, b in zip(pw, m_inv, pw_bd)]
            m_inv = [m + x + b[L:2 * L] for m, x, b in zip(m_inv, p32, both)]
            p32 = [b[0:L] for b in both]
        else:
            m_inv = [m + x + _dot(m.astype(BF16), b) for m, x, b in zip(m_inv, p32, pw_bd)]
    m_bd = [block_diag(m.astype(BF16)) for m in m_inv]
    x1 = [r + _dot(m, r.astype(BF16)) for m, r in zip(m_bd, rhs)]
    res = []
    for r, x, a, ah, ah_bd in zip(rhs, x1, neg_a, n_hi, n_bd):
        al_bd = block_diag((a - ah.astype(F32)).astype(BF16))
        xh = x.astype(BF16)
        xl = (x - xh.astype(F32)).astype(BF16)
        hi = _dot(jnp.concatenate([ah_bd, al_bd], axis=0), xh)
        res.append((r - x) + (hi[0:2 * L] + hi[2 * L:4 * L] + _dot(ah_bd, xl)))
    sol = [x + r + _dot(m, r.astype(BF16)) for x, r, m in zip(x1, res, m_bd)]

    heads = range(nh)
    rows = [slice((hv % 2) * L, (hv % 2 + 1) * L) for hv in heads]
    S = [s_s[hv] for hv in heads]
    Sb = [x.astype(BF16) for x in S]
    wq = [_dot(jnp.concatenate([sol[hv // 2][rows[hv], 0:GD_DK].astype(BF16), qb[hv // 2]], axis=0), Sb[hv])
          for hv in heads]
    ub = [jnp.concatenate([sol[p][rows[2 * p + r], GD_DK:GD_DK + GD_DV] - wq[2 * p + r][0:L] for r in range(2)],
                          axis=0).astype(BF16) for p in pairs]
    pu = [_dot(block_diag(p_mat[p]), ub[p]) for p in pairs]
    kdec = [(jnp.exp(Gl_all[:, hv:hv + 1] - G_c[:, hv:hv + 1]) * k32[hv // 2]).astype(BF16) for hv in heads]
    ku = [_dot_tn(kdec[hv], ub[hv // 2][rows[hv]]) for hv in heads]
    for hv in heads:
        o = eG[hv // 2][rows[hv]] * wq[hv][L:2 * L] + pu[hv // 2][rows[hv]]
        o_ref[:, hv * GD_DV:(hv + 1) * GD_DV] = o.astype(o_ref.dtype)
        s_s[hv] = jnp.exp(Gl_all[:, hv:hv + 1]) * S[hv] + ku[hv]

    @pl.when(jnp.logical_and(is_tail, is_ctx))
    def _():
        sout_ref[...] = s_s[...]


def _gdn_scan(qkv, ab, cache_s, j_layer):
    HV, nh, L, S = GD_V_HEADS, GD_GV, GD_CHUNK, GD_SLOTS
    a_d = ab[:, 0:2 * HV].reshape(T_ALL, 2, GD_GROUPS, nh)
    b_d = ab[:, 2 * HV:4 * HV].reshape(T_ALL, 2, GD_GROUPS, nh)
    gcol = jnp.concatenate([a_d, b_d], axis=-1).reshape(S.n, L, 2, GD_GROUPS, 2 * nh)
    gcol = gcol.transpose(2, 3, 0, 1, 4)
    grow = gcol.transpose(0, 1, 2, 4, 3)

    qb = GD_GQ * GD_DK
    vb = nh * GD_DV
    return pl.pallas_call(
        _gdn_kernel,
        grid=(2, GD_GROUPS, S.n),
        in_specs=[
            pl.BlockSpec((L, qb), lambda d, p, g: (S.slot(d, g), p)),
            pl.BlockSpec((L, qb), lambda d, p, g: (S.slot(d, g), GD_GROUPS + p)),
            pl.BlockSpec((L, vb), lambda d, p, g: (S.slot(d, g), GD_GROUPS + p)),
            pl.BlockSpec((None, None, None, L, 2 * nh), lambda d, p, g: (d, p, S.slot(d, g), 0, 0)),
            pl.BlockSpec((None, None, None, 2 * nh, L), lambda d, p, g: (d, p, S.slot(d, g), 0, 0)),
            pl.BlockSpec((None, None, None, nh, GD_DK, GD_DV),
                         lambda d, p, g: (S.lat_batch(d, g), j_layer, d, p, 0, 0)),
        ],
        out_specs=[
            pl.BlockSpec((None, L, vb), lambda d, p, g: (d, S.slot(d, g), p)),
            pl.BlockSpec((None, None, nh, GD_DK, GD_DV), lambda d, p, g: (S.ctx_seq(d, g), d, p, 0, 0)),
        ],
        out_shape=[
            jax.ShapeDtypeStruct((2, T_ALL, GD_INNER), BF16),
            jax.ShapeDtypeStruct((BATCH, 2, HV, GD_DK, GD_DV), F32),
        ],
        scratch_shapes=[pltpu.VMEM((nh, GD_DK, GD_DV), F32)],
        compiler_params=_cparams(3),
        name="gdn_scan",
    )(qkv, qkv, qkv, gcol, grow, cache_s)


def _gdn_post_kernel(of_ref, ob_ref, z_ref, g_ref, y_ref, *, tn):
    g = g_ref[...]
    for hh in range(tn // GD_DV):
        sl = slice(hh * GD_DV, (hh + 1) * GD_DV)
        o = of_ref[:, sl].astype(F32) + ob_ref[:, sl].astype(F32)
        r = lax.rsqrt(jnp.mean(o * o, axis=-1, keepdims=True) + EPS)
        y_ref[:, sl] = ((o * r * g) * _silu(z_ref[:, sl].astype(F32))).astype(BF16)


def _gdn_post(odir, z, g_norm):
    tm, tn = 1024, 512
    return pl.pallas_call(
        functools.partial(_gdn_post_kernel, tn=tn),
        grid=(T_ALL // tm, GD_INNER // tn),
        in_specs=[
            pl.BlockSpec((None, tm, tn), lambda i, j: (0, i, j)),
            pl.BlockSpec((None, tm, tn), lambda i, j: (1, i, j)),
            pl.BlockSpec((tm, tn), lambda i, j: (i, j)),
            pl.BlockSpec((1, GD_DV), lambda i, j: (0, 0)),
        ],
        out_specs=pl.BlockSpec((tm, tn), lambda i, j: (i, j)),
        out_shape=jax.ShapeDtypeStruct((T_ALL, GD_INNER), BF16),
        compiler_params=_cparams(2),
        name="gdn_gate_norm",
    )(odir, odir, z, g_norm.reshape(1, GD_DV))


def kernel(x_prompt, x_sample, c, cache_ml_C, cache_ml_n, cache_ml_m, cache_gd_S, c_ctx, w_ada, b_ada, g_norm,
           w_ml_in, b_ml_gate, g_ml_head, w_ml_out, w_sc_in, w_sc_conv, w_sc_out, w_gd_in, w_gd_conv, gd_A_log,
           gd_dt_bias, g_gd_norm, w_gd_out, g_final):
    xs = (x_prompt.reshape(T_CTX, D_MODEL), x_sample.reshape(T_LAT, D_MODEL))
    cond = jnp.concatenate([c_ctx[None, :], c, jnp.zeros((N_COND - 1 - DEC_BATCH, D_MODEL), F32)], axis=0)
    mod = _modulation(cond, w_ada, b_ada)
    w_ml_nk = jnp.swapaxes(w_ml_in, 1, 2)

    state_ml_c = None
    ml_n, ml_m, gd_s = [], [], []
    for l in range(DEPTH):
        shift = mod[l, :, 0:D_MODEL]
        scale = mod[l, :, D_MODEL:2 * D_MODEL]
        gate = mod[l, :, 2 * D_MODEL:3 * D_MODEL]
        h = _norm_mod(xs, g_norm[l], shift, scale)
        j = l // 3
        kind = l % 3
        if kind == 0:
            proj = _project(h, w_ml_nk, j, 0, ML_MAIN, 1024, BF16, w_is_nk=True)
            gates = _ml_gates(h, w_ml_nk, j, b_ml_gate[j])
            hdir, state_ml_c, n_fin, m_fin = _mlstm_scan(proj, gates, cache_ml_C, cache_ml_n, cache_ml_m, j,
                                                         state_ml_c)
            ml_n.append(n_fin)
            ml_m.append(m_fin[..., 0])
            y = _mlstm_post(hdir, proj, g_ml_head[j])
            xs = (_project_residual(y, w_ml_out, j, xs, gate),)
        elif kind == 1:
            y = _shortconv(h, w_sc_in, j, w_sc_conv)
            xs = (_project_residual(y, w_sc_out, j, xs, gate),)
        else:
            qkv = _gdn_qkv(h, w_gd_in, j, w_gd_conv)
            z = _project(h, w_gd_in, j, GD_CONV_CH, GD_INNER, 1024, BF16)
            ab = _gd_gates(h, w_gd_in, j, gd_A_log[j], gd_dt_bias[j])
            odir, s_fin = _gdn_scan(qkv, ab, cache_gd_S, j)
            gd_s.append(s_fin)
            y = _gdn_post(odir, z, g_gd_norm[j])
            xs = (_project_residual(y, w_gd_out, j, xs, gate),)

    y_ctx, y_lat = _final_norm(xs[0], g_final)
    y_prompt = y_ctx.reshape(BATCH, SEQ, D_MODEL)
    y_sample = y_lat.reshape(DEC_BATCH, DEC_SEQ, D_MODEL)
    state_ml_n = jnp.stack(ml_n, axis=1)
    state_ml_m = jnp.stack(ml_m, axis=1)
    state_gd_s = jnp.stack(gd_s, axis=1)
    return (y_prompt, y_sample, state_ml_c, state_ml_n, state_ml_m, state_gd_s)
```

```python
import functools

import jax
import jax.numpy as jnp
from jax import lax
from jax.experimental import pallas as pl
from jax.experimental.pallas import tpu as pltpu

F32 = jnp.float32
BF16 = jnp.bfloat16
HIGHEST = lax.Precision.HIGHEST

D_MODEL = 2048
BATCH = 16
SEQ = 256
DEPTH = 4
DEC_BATCH = 2
DEC_SEQ = 2048
GRID_W = 64
EPS = 1e-6

T_CTX = BATCH * SEQ
T_LAT = DEC_BATCH * DEC_SEQ
T_ALL = T_CTX + T_LAT
N_COND = 8

ML_HEADS = 8
ML_DK = 256
ML_DV = 512
ML_QK = ML_HEADS * ML_DK
ML_INNER = ML_HEADS * ML_DV
ML_MAIN = 2 * ML_QK + 3 * ML_INNER
N_ML = 2
ML_CHUNK = SEQ
ML_LAT_CHUNKS = DEC_SEQ // ML_CHUNK
ML_LAT_SLOTS = DEC_BATCH * ML_LAT_CHUNKS
ML_CBLK = 256

SC_INNER = 2 * D_MODEL

GD_DK = 128
GD_DV = 128
GD_QK_HEADS = 16
GD_V_HEADS = 32
GD_QK = GD_QK_HEADS * GD_DK
GD_INNER = GD_V_HEADS * GD_DV
GD_CONV_CH = 2 * GD_QK + GD_INNER
GD_MAIN = GD_CONV_CH + GD_INNER
GD_GROUPS = 1
GD_GV = GD_V_HEADS // GD_GROUPS
GD_GQ = GD_QK_HEADS // GD_GROUPS
GD_CHUNK = 64

LANE = 128
BF16_ROWS = 16
VMEM_LIMIT = 56 * 1024 * 1024


def _cparams(n_axes):
    return pltpu.CompilerParams(dimension_semantics=("arbitrary",) * n_axes, vmem_limit_bytes=VMEM_LIMIT)


def _sigmoid(x):
    return 0.5 * jnp.tanh(0.5 * x) + 0.5


def _silu(x):
    h = 0.5 * x
    return h + h * jnp.tanh(h)


def _softplus(x):
    return jnp.maximum(x, 0.0) + jnp.log1p(jnp.exp(-jnp.abs(x)))


def _log_sigmoid(x):
    return -_softplus(-x)


def _cond_group(row0):
    return jnp.where(row0 < T_CTX, 0, 1 + (row0 - T_CTX) // DEC_SEQ)


def _conv_row_len(row0):
    return jnp.where(row0 < T_CTX, SEQ, GRID_W)


def _dot_nt(a, b, **kw):
    return lax.dot_general(a, b, (((1,), (1,)), ((), ())), preferred_element_type=F32, **kw)


def _dot_tn(a, b, **kw):
    return lax.dot_general(a, b, (((0,), (0,)), ((), ())), preferred_element_type=F32, **kw)


def _dot(a, b, **kw):
    return jnp.dot(a, b, preferred_element_type=F32, **kw)


def _mod_kernel(s_ref, w_ref, b_ref, o_ref):
    a = _silu(s_ref[...]).astype(BF16)
    o_ref[...] = _dot(a, w_ref[...].astype(BF16)) + b_ref[...]


def _modulation(cond, w_ada, b_ada):
    tn = 1024
    n = 3 * D_MODEL
    return pl.pallas_call(
        _mod_kernel,
        grid=(DEPTH, n // tn),
        in_specs=[
            pl.BlockSpec((N_COND, D_MODEL), lambda l, j: (0, 0)),
            pl.BlockSpec((None, D_MODEL, tn), lambda l, j: (l, 0, j)),
            pl.BlockSpec((None, 1, tn), lambda l, j: (l, 0, j)),
        ],
        out_specs=pl.BlockSpec((None, N_COND, tn), lambda l, j: (l, 0, j)),
        out_shape=jax.ShapeDtypeStruct((DEPTH, N_COND, n), F32),
        compiler_params=_cparams(2),
        name="adaln_modulation",
    )(cond, w_ada, b_ada.reshape(DEPTH, 1, n))


def _split_specs(block, index_map, tm):
    n_ctx = T_CTX // tm

    def ctx_map(*idx):
        row, *rest = index_map(*idx)
        return (jnp.minimum(row, n_ctx - 1), *rest)

    def lat_map(*idx):
        row, *rest = index_map(*idx)
        return (jnp.maximum(row - n_ctx, 0), *rest)

    return [pl.BlockSpec(block, ctx_map), pl.BlockSpec(block, lat_map)]


def _token_tile(refs, row0):
    if len(refs) == 1:
        return refs[0][...]
    return jnp.where(row0 < T_CTX, refs[0][...], refs[1][...])


def _norm_mod_kernel(*refs, tm):
    *x_refs, g_ref, shift_ref, scale_ref, o_ref = refs
    row0 = pl.program_id(0) * tm
    grp = _cond_group(row0)
    x = _token_tile(x_refs, row0)
    r = lax.rsqrt(jnp.mean(x * x, axis=-1, keepdims=True) + EPS)
    xn = x * r * g_ref[...]
    sh = shift_ref[pl.ds(grp, 1), :]
    sc = scale_ref[pl.ds(grp, 1), :]
    o_ref[...] = (xn * (1.0 + sc) + sh).astype(BF16)


def _norm_mod(xs, g, shift, scale):
    tm = 512
    x_map = lambda i: (i, 0)
    x_specs = [pl.BlockSpec((tm, D_MODEL), x_map)] if len(xs) == 1 else _split_specs((tm, D_MODEL), x_map, tm)
    return pl.pallas_call(
        functools.partial(_norm_mod_kernel, tm=tm),
        grid=(T_ALL // tm,),
        in_specs=[
            *x_specs,
            pl.BlockSpec((1, D_MODEL), lambda i: (0, 0)),
            pl.BlockSpec((N_COND, D_MODEL), lambda i: (0, 0)),
            pl.BlockSpec((N_COND, D_MODEL), lambda i: (0, 0)),
        ],
        out_specs=pl.BlockSpec((tm, D_MODEL), lambda i: (i, 0)),
        out_shape=jax.ShapeDtypeStruct((T_ALL, D_MODEL), BF16),
        compiler_params=_cparams(1),
        name="norm_modulate",
    )(*xs, g.reshape(1, D_MODEL), shift, scale)


def _final_norm_kernel(x_ref, g_ref, ctx_ref, lat_ref, *, ctx_tiles):
    x = x_ref[...]
    r = lax.rsqrt(jnp.mean(x * x, axis=-1, keepdims=True) + EPS)
    y = x * r * g_ref[...]
    i = pl.program_id(0)

    @pl.when(i < ctx_tiles)
    def _():
        ctx_ref[...] = y

    @pl.when(i >= ctx_tiles)
    def _():
        lat_ref[...] = y


def _final_norm(x, g):
    tm = 512
    ctx_tiles = T_CTX // tm
    return pl.pallas_call(
        functools.partial(_final_norm_kernel, ctx_tiles=ctx_tiles),
        grid=(T_ALL // tm,),
        in_specs=[
            pl.BlockSpec((tm, D_MODEL), lambda i: (i, 0)),
            pl.BlockSpec((1, D_MODEL), lambda i: (0, 0)),
        ],
        out_specs=[
            pl.BlockSpec((tm, D_MODEL), lambda i: (jnp.minimum(i, ctx_tiles - 1), 0)),
            pl.BlockSpec((tm, D_MODEL), lambda i: (jnp.maximum(i - ctx_tiles, 0), 0)),
        ],
        out_shape=[
            jax.ShapeDtypeStruct((T_CTX, D_MODEL), F32),
            jax.ShapeDtypeStruct((T_LAT, D_MODEL), F32),
        ],
        compiler_params=_cparams(1),
        name="final_norm",
    )(x, g.reshape(1, D_MODEL))


def _proj_kernel(a_ref, w_ref, o_ref, wb_ref, *, w_is_nk):
    @pl.when(pl.program_id(1) == 0)
    def _():
        wb_ref[...] = w_ref[...].astype(BF16)

    dot = _dot_nt if w_is_nk else _dot
    o_ref[...] = dot(a_ref[...], wb_ref[...]).astype(o_ref.dtype)


def _project(a, w, layer, col0, n_cols, tn, out_dtype, w_is_nk=False, tm=2048):
    k = a.shape[1]
    j0 = col0 // tn
    if w_is_nk:
        w_spec = pl.BlockSpec((None, tn, k), lambda j, i: (layer, j0 + j, 0))
        w_tile = (tn, k)
    else:
        w_spec = pl.BlockSpec((None, k, tn), lambda j, i: (layer, 0, j0 + j))
        w_tile = (k, tn)
    return pl.pallas_call(
        functools.partial(_proj_kernel, w_is_nk=w_is_nk),
        grid=(n_cols // tn, T_ALL // tm),
        in_specs=[pl.BlockSpec((tm, k), lambda j, i: (i, 0)), w_spec],
        out_specs=pl.BlockSpec((tm, tn), lambda j, i: (i, j)),
        out_shape=jax.ShapeDtypeStruct((T_ALL, n_cols), out_dtype),
        scratch_shapes=[pltpu.VMEM(w_tile, BF16)],
        compiler_params=_cparams(2),
        name="in_projection",
    )(a, w)


def _proj_res_kernel(a_ref, w_ref, *refs, tm):
    *x_refs, gate_ref, o_ref, wb_ref = refs
    i = pl.program_id(1)

    @pl.when(i == 0)
    def _():
        wb_ref[...] = w_ref[...].astype(BF16)

    gate = gate_ref[pl.ds(_cond_group(i * tm), 1), :]
    o_ref[...] = _token_tile(x_refs, i * tm) + gate * _dot(a_ref[...], wb_ref[...])


def _project_residual(a, w, layer, xs, gate):
    k = a.shape[1]
    tm, tn = 1024, 512
    x_map = lambda j, i: (i, j)
    x_specs = [pl.BlockSpec((tm, tn), x_map)] if len(xs) == 1 else _split_specs((tm, tn), x_map, tm)
    return pl.pallas_call(
        functools.partial(_proj_res_kernel, tm=tm),
        grid=(D_MODEL // tn, T_ALL // tm),
        in_specs=[
            pl.BlockSpec((tm, k), lambda j, i: (i, 0)),
            pl.BlockSpec((None, k, tn), lambda j, i: (layer, 0, j)),
            *x_specs,
            pl.BlockSpec((N_COND, tn), lambda j, i: (0, j)),
        ],
        out_specs=pl.BlockSpec((tm, tn), lambda j, i: (i, j)),
        out_shape=jax.ShapeDtypeStruct((T_ALL, D_MODEL), F32),
        scratch_shapes=[pltpu.VMEM((k, tn), BF16)],
        compiler_params=_cparams(2),
        name="out_projection_residual",
    )(a, w, *xs, gate)


def _chunk_cumsums(x_fwd, x_bwd, chunk):
    r = lax.broadcasted_iota(jnp.int32, (chunk, chunk), 0)
    c = lax.broadcasted_iota(jnp.int32, (chunk, chunk), 1)
    lower = (c <= r).astype(F32)
    upper = (c >= r).astype(F32)
    rows = [slice(i * chunk, (i + 1) * chunk) for i in range(x_fwd.shape[0] // chunk)]
    fwd = jnp.concatenate([_dot(lower, x_fwd[sl], precision=HIGHEST) for sl in rows], axis=0)
    bwd = jnp.concatenate([_dot(upper, x_bwd[sl], precision=HIGHEST) for sl in rows], axis=0)
    return fwd, bwd


def _ml_gate_kernel(a_ref, w_ref, b_ref, o_ref):
    H = ML_HEADS
    x = _dot_nt(a_ref[...], w_ref[...].astype(BF16)) + b_ref[...]
    lf = _log_sigmoid(x[:, 2 * H:4 * H])
    o_ref[:, 0:2 * H] = x[:, 0:2 * H]
    o_ref[:, 2 * H:3 * H], o_ref[:, 3 * H:4 * H] = _chunk_cumsums(lf[:, 0:H], lf[:, H:2 * H], ML_CHUNK)


def _ml_gates(a, w_nk, layer, b_gate):
    tm, n = 512, 4 * ML_HEADS
    k = a.shape[1]
    return pl.pallas_call(
        _ml_gate_kernel,
        grid=(T_ALL // tm,),
        in_specs=[
            pl.BlockSpec((tm, k), lambda i: (i, 0)),
            pl.BlockSpec((None, n, k), lambda i: (layer, ML_MAIN // n, 0)),
            pl.BlockSpec((1, n), lambda i: (0, 0)),
        ],
        out_specs=pl.BlockSpec((tm, n), lambda i: (i, 0)),
        out_shape=jax.ShapeDtypeStruct((T_ALL, n), F32),
        compiler_params=_cparams(1),
        name="mlstm_gates",
    )(a, w_nk, b_gate.reshape(1, n))


def _gd_gate_kernel(a_ref, w_ref, p_ref, o_ref):
    HV = GD_V_HEADS
    x = _dot(a_ref[...], w_ref[...].astype(BF16))
    g = -jnp.exp(p_ref[0:1, :]) * _softplus(x[:, 0:2 * HV] + p_ref[1:2, :])
    o_ref[:, 0:HV], o_ref[:, HV:2 * HV] = _chunk_cumsums(g[:, 0:HV], g[:, HV:2 * HV], GD_CHUNK)
    o_ref[:, 2 * HV:4 * HV] = _sigmoid(x[:, 2 * HV:4 * HV])


def _gd_gates(a, w_in, layer, a_log, dt_bias):
    tm, n = 512, 4 * GD_V_HEADS
    k = a.shape[1]
    par = jnp.stack([a_log.astype(F32).reshape(-1), dt_bias.astype(F32).reshape(-1)], axis=0)
    return pl.pallas_call(
        _gd_gate_kernel,
        grid=(T_ALL // tm,),
        in_specs=[
            pl.BlockSpec((tm, k), lambda i: (i, 0)),
            pl.BlockSpec((None, k, n), lambda i: (layer, 0, GD_MAIN // n)),
            pl.BlockSpec((2, n // 2), lambda i: (0, 0)),
        ],
        out_specs=pl.BlockSpec((tm, n), lambda i: (i, 0)),
        out_shape=jax.ShapeDtypeStruct((T_ALL, n), F32),
        compiler_params=_cparams(1),
        name="gdn_gates",
    )(a, w_in, par)


class _Slots:
    def __init__(self, chunk):
        self.chunk = chunk
        self.n = T_ALL // chunk
        self.ctx = T_CTX // chunk
        self.ctx_chunks = SEQ // chunk
        self.lat_chunks = DEC_SEQ // chunk

    def slot(self, d, g):
        return g + d * (self.n - 1 - 2 * g)

    def seq(self, j):
        return jnp.where(j < self.ctx, j // self.ctx_chunks, BATCH + (j - self.ctx) // self.lat_chunks)

    def flags(self, d, j):
        is_ctx = j < self.ctx
        pos = jnp.where(is_ctx, j % self.ctx_chunks, (j - self.ctx) % self.lat_chunks)
        n_chunks = jnp.where(is_ctx, self.ctx_chunks, self.lat_chunks)
        head_pos = jnp.where(d == 0, 0, n_chunks - 1)
        tail_pos = jnp.where(d == 0, n_chunks - 1, 0)
        return is_ctx, pos == head_pos, pos == tail_pos

    def mask(self, d):
        t_i = lax.broadcasted_iota(jnp.int32, (self.chunk, self.chunk), 0)
        s_i = lax.broadcasted_iota(jnp.int32, (self.chunk, self.chunk), 1)
        diff = (t_i - s_i) * (1 - 2 * d)
        return diff >= 0, diff > 0

    def lat_batch(self, d, g):
        return jnp.clip(self.seq(self.slot(d, g)) - BATCH, 0, DEC_BATCH - 1)

    def ctx_seq(self, d, g):
        return jnp.minimum(self.seq(self.slot(d, g)), BATCH - 1)


GD_SLOTS = _Slots(GD_CHUNK)


def _mlstm_kernel(*refs, recurrent, n_alias, zero_layers=0):
    q_ref, k_ref, v_ref, gc_ref, gr_ref, *rest = refs
    if recurrent:
        c0_ref, n0_ref, m0_ref, *rest = rest
    rest = rest[n_alias:]
    if recurrent:
        h_ref, c_s, n_s, m_s = rest
    else:
        h_ref, cout_ref, nout_ref, mout_ref = rest
    L = ML_CHUNK
    d = pl.program_id(0)

    if recurrent:
        pos = _lat_slot(d, pl.program_id(1)) % ML_LAT_CHUNKS

        @pl.when(pos == jnp.where(d == 0, 0, ML_LAT_CHUNKS - 1))
        def _():
            c_s[...] = c0_ref[...]
            n_s[...] = n0_ref[...]
            m_s[...] = m0_ref[...]

    t_i = lax.broadcasted_iota(jnp.int32, (L, L), 0)
    s_i = lax.broadcasted_iota(jnp.int32, (L, L), 1)
    mask = (t_i - s_i) * (1 - 2 * d) >= 0
    gc = gc_ref[...]
    gr = gr_ref[...]
    i_c = gc[:, 0:ML_HEADS]
    b_c = gc[:, ML_HEADS:2 * ML_HEADS]
    i_r = gr[0:ML_HEADS, :]
    b_r = gr[ML_HEADS:2 * ML_HEADS, :]
    bl_all = jnp.where(d == 0, b_c[L - 1:L, :], b_c[0:1, :])

    qbs = [q_ref[:, h * ML_DK:(h + 1) * ML_DK] * (ML_DK ** -0.5) for h in range(ML_HEADS)]
    a_all, m_new_all, dec_all, z_all, mt_all, sc_all = [], [], [], [], [], []
    for h in range(ML_HEADS):
        bcol = b_c[:, h:h + 1]
        m = m_s[h:h + 1, 0:1] if recurrent else 0.0
        z = jnp.where(mask, i_r[h:h + 1, :] - b_r[h:h + 1, :], -jnp.inf)
        inter = bcol + m
        mt = jnp.maximum(inter, bcol + jnp.max(z, axis=-1, keepdims=True))
        z_all.append(bcol - mt)
        mt_all.append(mt)
        bl = bl_all[:, h:h + 1]
        a = bl - bcol + i_c[:, h:h + 1]
        m_new = jnp.maximum(bl + m, jnp.max(a, axis=0, keepdims=True))
        a_all.append(a)
        m_new_all.append(m_new)
        if recurrent:
            sc_all.append(jnp.exp(inter - mt))
            dec_all.append(jnp.exp(bl + m - m_new))
    qkn = []
    for h in range(ML_HEADS):
        keys = k_ref[:, h * ML_DK:(h + 1) * ML_DK]
        if recurrent:
            nb = jnp.broadcast_to(n_s[h:h + 1, :], (BF16_ROWS, ML_DK)).astype(BF16)
            keys = jnp.concatenate([keys, nb], axis=0)
        qkn.append(_dot_nt(qbs[h], keys))
    lhs, rinv = [], []
    for h in range(ML_HEADS):
        z = jnp.where(mask, i_r[h:h + 1, :] - b_r[h:h + 1, :], -jnp.inf)
        s = qkn[h][:, 0:L] * jnp.exp(z + z_all[h])
        den = jnp.sum(s, axis=-1, keepdims=True)
        if recurrent:
            den = sc_all[h] * qkn[h][:, L:L + 1] + den
            lhs.append(jnp.concatenate([(qbs[h].astype(F32) * sc_all[h]).astype(BF16), s.astype(BF16)], axis=1))
        else:
            lhs.append(s.astype(BF16))
        rinv.append(1.0 / jnp.maximum(jnp.abs(den), jnp.exp(-mt_all[h])))
    for h in range(ML_HEADS):
        w = v_ref[:, h * ML_DV:(h + 1) * ML_DV]
        if recurrent:
            w = jnp.concatenate([c_s[h].astype(BF16), w], axis=0)
        h_ref[:, h * ML_DV:(h + 1) * ML_DV] = (_dot(lhs[h], w) * rinv[h]).astype(h_ref.dtype)

    for h in range(ML_HEADS):
        wk = jnp.exp(a_all[h] - m_new_all[h]) * k_ref[:, h * ML_DK:(h + 1) * ML_DK].astype(F32)
        wkb = wk.astype(BF16)
        n_add = jnp.sum(wk, axis=0, keepdims=True)
        m_row = jnp.broadcast_to(m_new_all[h], (1, LANE))
        for cb in range(ML_DV // ML_CBLK):
            sl = slice(h * ML_DV + cb * ML_CBLK, h * ML_DV + (cb + 1) * ML_CBLK)
            csl = slice(cb * ML_CBLK, (cb + 1) * ML_CBLK)
            upd = _dot_tn(wkb, v_ref[:, sl])
            if recurrent:
                c_s[h, :, csl] = dec_all[h] * c_s[h, :, csl] + upd
            else:
                if zero_layers:
                    cout_ref[0, h, :, csl] = upd
                    for extra in range(zero_layers):
                        cout_ref[1 + extra, h, :, csl] = jnp.zeros_like(upd)
                else:
                    cout_ref[h, :, csl] = upd
        if recurrent:
            n_s[h:h + 1, :] = dec_all[h] * n_s[h:h + 1, :] + n_add
            m_s[h:h + 1, :] = m_row
        else:
            nout_ref[h:h + 1, :] = n_add
            mout_ref[h:h + 1, :] = m_row


def _lat_slot(d, g):
    return g + d * (ML_LAT_SLOTS - 1 - 2 * g)


def _mlstm_scan(proj, gates, cache_c, cache_n, cache_m, j_layer, c_acc):
    H, L = ML_HEADS, ML_CHUNK
    n_slot = T_ALL // L
    gd = jnp.stack([jnp.concatenate([gates[:, 0:H], gates[:, 2 * H:3 * H]], axis=1),
                    jnp.concatenate([gates[:, H:2 * H], gates[:, 3 * H:4 * H]], axis=1)], axis=0)
    gcol = gd.reshape(2, n_slot, L, 2 * H)
    grow = gcol.transpose(0, 1, 3, 2)
    m0 = jnp.broadcast_to(cache_m[:, j_layer][..., None], (DEC_BATCH, 2, H, LANE))

    def token_specs(blk):
        return [
            pl.BlockSpec((L, ML_QK), lambda d, g: (blk(d, g), 0)),
            pl.BlockSpec((L, ML_QK), lambda d, g: (blk(d, g), 1)),
            pl.BlockSpec((L, ML_INNER), lambda d, g: (blk(d, g), 1)),
            pl.BlockSpec((None, None, L, 2 * H), lambda d, g: (d, blk(d, g), 0, 0)),
            pl.BlockSpec((None, None, 2 * H, L), lambda d, g: (d, blk(d, g), 0, 0)),
        ]

    ctx_specs = token_specs(lambda d, g: g)
    ctx_args = [proj, proj, proj, gcol, grow]
    aliases = {}
    if c_acc is None:
        assert j_layer == 0
        zero_layers = N_ML - 1
        c_spec = pl.BlockSpec((None, N_ML, None, H, ML_DK, ML_DV), lambda d, g: (g, 0, d, 0, 0, 0))
    else:
        zero_layers = 0
        c_spec = pl.BlockSpec((None, None, None, H, ML_DK, ML_DV), lambda d, g: (g, j_layer, d, 0, 0, 0))
        ctx_specs.append(pl.BlockSpec(memory_space=pl.ANY))
        ctx_args.append(c_acc)
        aliases = {len(ctx_args) - 1: 1}
    h_ctx, c_fin, n_fin, m_fin = pl.pallas_call(
        functools.partial(_mlstm_kernel, recurrent=False, n_alias=len(aliases), zero_layers=zero_layers),
        grid=(2, BATCH),
        in_specs=ctx_specs,
        out_specs=[
            pl.BlockSpec((None, L, ML_INNER), lambda d, g: (d, g, 0)),
            c_spec,
            pl.BlockSpec((None, None, H, ML_DK), lambda d, g: (g, d, 0, 0)),
            pl.BlockSpec((None, None, H, LANE), lambda d, g: (g, d, 0, 0)),
        ],
        out_shape=[
            jax.ShapeDtypeStruct((2, T_CTX, ML_INNER), BF16),
            jax.ShapeDtypeStruct((BATCH, N_ML, 2, H, ML_DK, ML_DV), F32),
            jax.ShapeDtypeStruct((BATCH, 2, H, ML_DK), F32),
            jax.ShapeDtypeStruct((BATCH, 2, H, LANE), F32),
        ],
        input_output_aliases=aliases,
        compiler_params=_cparams(2),
        name="mlstm_context",
    )(*ctx_args)

    lat_blk = lambda d, g: BATCH + _lat_slot(d, g)
    lat_b = lambda d, g: _lat_slot(d, g) // ML_LAT_CHUNKS
    lat_specs = token_specs(lat_blk) + [
        pl.BlockSpec((None, None, None, H, ML_DK, ML_DV), lambda d, g: (lat_b(d, g), j_layer, d, 0, 0, 0)),
        pl.BlockSpec((None, None, None, H, ML_DK), lambda d, g: (lat_b(d, g), j_layer, d, 0, 0)),
        pl.BlockSpec((None, None, H, LANE), lambda d, g: (lat_b(d, g), d, 0, 0)),
    ]
    h_lat = pl.pallas_call(
        functools.partial(_mlstm_kernel, recurrent=True, n_alias=0),
        grid=(2, ML_LAT_SLOTS),
        in_specs=lat_specs,
        out_specs=pl.BlockSpec((None, L, ML_INNER), lambda d, g: (d, _lat_slot(d, g), 0)),
        out_shape=jax.ShapeDtypeStruct((2, T_LAT, ML_INNER), BF16),
        scratch_shapes=[
            pltpu.VMEM((H, ML_DK, ML_DV), F32),
            pltpu.VMEM((H, ML_DK), F32),
            pltpu.VMEM((H, LANE), F32),
        ],
        compiler_params=_cparams(2),
        name="mlstm_latent",
    )(proj, proj, proj, gcol, grow, cache_c, cache_n, m0)
    return (h_ctx, h_lat), c_fin, n_fin, m_fin


def _mlstm_post_kernel(hfc_ref, hfl_ref, hbc_ref, hbl_ref, o_ref, z_ref, g_ref, y_ref, *, tm):
    row0 = pl.program_id(0) * tm
    hs = _token_tile((hfc_ref, hfl_ref), row0).astype(F32) + _token_tile((hbc_ref, hbl_ref), row0).astype(F32)
    r = lax.rsqrt(jnp.mean(hs * hs, axis=-1, keepdims=True) + EPS)
    y = (hs * r * g_ref[...]) * _sigmoid(o_ref[...].astype(F32)) * _silu(z_ref[...].astype(F32))
    y_ref[...] = y.astype(BF16)


def _mlstm_post(hdirs, proj, g_head):
    tm = 1024
    n_ctx = T_CTX // tm
    o_blk = (2 * ML_QK + ML_INNER) // ML_DV
    z_blk = (2 * ML_QK + 2 * ML_INNER) // ML_DV
    h_ctx, h_lat = hdirs
    h_specs = []
    for d in range(2):
        h_specs.append(pl.BlockSpec((None, tm, ML_DV), lambda i, h, d=d: (d, jnp.minimum(i, n_ctx - 1), h)))
        h_specs.append(pl.BlockSpec((None, tm, ML_DV), lambda i, h, d=d: (d, jnp.maximum(i - n_ctx, 0), h)))
    return pl.pallas_call(
        functools.partial(_mlstm_post_kernel, tm=tm),
        grid=(T_ALL // tm, ML_HEADS),
        in_specs=[
            *h_specs,
            pl.BlockSpec((tm, ML_DV), lambda i, h: (i, o_blk + h)),
            pl.BlockSpec((tm, ML_DV), lambda i, h: (i, z_blk + h)),
            pl.BlockSpec((1, ML_DV), lambda i, h: (0, h)),
        ],
        out_specs=pl.BlockSpec((tm, ML_DV), lambda i, h: (i, h)),
        out_shape=jax.ShapeDtypeStruct((T_ALL, ML_INNER), BF16),
        compiler_params=_cparams(2),
        name="mlstm_gate_norm",
    )(h_ctx, h_lat, h_ctx, h_lat, proj, proj, g_head.reshape(1, ML_INNER))


def _dwconv3_tile(x, w_ref, row_len, tm):
    r = lax.broadcasted_iota(jnp.int32, (tm, 1), 0) & (row_len - 1)
    prev = jnp.where(r == 0, 0.0, pltpu.roll(x, 1, axis=0))
    nxt = jnp.where(r == row_len - 1, 0.0, pltpu.roll(x, tm - 1, axis=0))
    return prev * w_ref[0:1, :] + x * w_ref[1:2, :] + nxt * w_ref[2:3, :]


def _sc_kernel(a_ref, wu_ref, wb_ref, wc_ref, wz_ref, cw_ref, y_ref, wbuf, *, tm):
    i = pl.program_id(1)

    @pl.when(i == 0)
    def _():
        for g, w_ref in enumerate((wu_ref, wb_ref, wc_ref, wz_ref)):
            wbuf[g] = w_ref[...].astype(BF16)

    a = a_ref[...]
    u, b, c, z = (_dot(a, wbuf[g]) for g in range(4))
    conv = _dwconv3_tile(c * u, cw_ref, _conv_row_len(i * tm), tm)
    y_ref[...] = (b * conv * _silu(z)).astype(BF16)


def _shortconv(a, w_in, layer, w_conv):
    tm, tn = 2048, 256
    k = a.shape[1]
    nb = SC_INNER // tn
    w_specs = [pl.BlockSpec((None, k, tn), functools.partial(lambda j, i, g: (layer, 0, g * nb + j), g=g))
               for g in range(4)]
    return pl.pallas_call(
        functools.partial(_sc_kernel, tm=tm),
        grid=(nb, T_ALL // tm),
        in_specs=[pl.BlockSpec((tm, k), lambda j, i: (i, 0)), *w_specs,
                  pl.BlockSpec((None, 3, tn), lambda j, i: (layer, 0, j))],
        out_specs=pl.BlockSpec((tm, tn), lambda j, i: (i, j)),
        out_shape=jax.ShapeDtypeStruct((T_ALL, SC_INNER), BF16),
        scratch_shapes=[pltpu.VMEM((4, k, tn), BF16)],
        compiler_params=_cparams(2),
        name="shortconv_mixer",
    )(a, w_in, w_in, w_in, w_in, w_conv)


def _gdn_qkv_kernel(a_ref, w_ref, cw_ref, o_ref, wb_ref, *, tm, tn):
    j = pl.program_id(0)
    i = pl.program_id(1)

    @pl.when(i == 0)
    def _():
        wb_ref[...] = w_ref[...].astype(BF16)

    y = _silu(_dwconv3_tile(_dot(a_ref[...], wb_ref[...]), cw_ref, _conv_row_len(i * tm), tm))
    q_blocks = GD_QK // tn
    is_qk = j < 2 * q_blocks
    q_scale = jnp.where(j < q_blocks, GD_DK ** -0.5, 1.0)
    for hh in range(tn // GD_DK):
        yh = y[:, hh * GD_DK:(hh + 1) * GD_DK]
        r = lax.rsqrt(jnp.sum(yh * yh, axis=-1, keepdims=True) + EPS) * q_scale
        o_ref[:, hh * GD_DK:(hh + 1) * GD_DK] = (yh * jnp.where(is_qk, r, 1.0)).astype(o_ref.dtype)


def _gdn_qkv(a, w_in, layer, w_conv):
    tm, tn = 1024, 1024
    k = a.shape[1]
    return pl.pallas_call(
        functools.partial(_gdn_qkv_kernel, tm=tm, tn=tn),
        grid=(GD_CONV_CH // tn, T_ALL // tm),
        in_specs=[
            pl.BlockSpec((tm, k), lambda j, i: (i, 0)),
            pl.BlockSpec((None, k, tn), lambda j, i: (layer, 0, j)),
            pl.BlockSpec((None, 3, tn), lambda j, i: (layer, 0, j)),
        ],
        out_specs=pl.BlockSpec((tm, tn), lambda j, i: (i, j)),
        out_shape=jax.ShapeDtypeStruct((T_ALL, GD_CONV_CH), BF16),
        scratch_shapes=[pltpu.VMEM((k, tn), BF16)],
        compiler_params=_cparams(2),
        name="gdn_qkv_projection",
    )(a, w_in, w_conv)


def _gdn_kernel(q_ref, k_ref, v_ref, gc_ref, gr_ref, s0_ref, o_ref, sout_ref, s_s):
    L = GD_CHUNK
    d = pl.program_id(0)
    j = GD_SLOTS.slot(d, pl.program_id(2))
    is_ctx, is_head, is_tail = GD_SLOTS.flags(d, j)

    @pl.when(jnp.logical_and(is_head, is_ctx))
    def _():
        s_s[...] = jnp.zeros_like(s_s)

    @pl.when(jnp.logical_and(is_head, jnp.logical_not(is_ctx)))
    def _():
        s_s[...] = s0_ref[...]

    nh = GD_GV
    G_c = gc_ref[:, 0:nh]
    beta_c = gc_ref[:, nh:2 * nh]
    G_r = gr_ref[0:nh, :]
    Gl_all = jnp.where(d == 0, G_c[L - 1:L, :], G_c[0:1, :])

    pairs = range(GD_GQ)
    lane_lo = lax.broadcasted_iota(jnp.int32, (L, 2 * L), 1) < L
    t_i = lax.broadcasted_iota(jnp.int32, (L, 2 * L), 0)
    s_i = lax.broadcasted_iota(jnp.int32, (L, 2 * L), 1) & (L - 1)
    diff = (t_i - s_i) * (1 - 2 * d)
    incl2, strict2 = diff >= 0, diff > 0

    def block_diag(x):
        zero = jnp.zeros_like(x)
        return jnp.concatenate([jnp.where(lane_lo, x, zero), jnp.where(lane_lo, zero, x)], axis=0)

    def stack_cols(a, p):
        return jnp.concatenate([a[:, 2 * p:2 * p + 1], a[:, 2 * p + 1:2 * p + 2]], axis=0)

    qb, k32, kkqk = [], [], []
    for p in pairs:
        kb = k_ref[:, p * GD_DK:(p + 1) * GD_DK]
        qb.append(q_ref[:, p * GD_DK:(p + 1) * GD_DK])
        k32.append(kb.astype(F32))
        kkqk.append(_dot_nt(jnp.concatenate([kb, qb[p]], axis=0), jnp.concatenate([kb, kb], axis=0)))

    neg_a, p_mat, rhs, eG = [], [], [], []
    for p in pairs:
        Gc2 = jnp.where(lane_lo, G_c[:, 2 * p:2 * p + 1], G_c[:, 2 * p + 1:2 * p + 2])
        Gr2 = jnp.concatenate([G_r[2 * p:2 * p + 1, :], G_r[2 * p + 1:2 * p + 2, :]], axis=1)
        beta2 = jnp.where(lane_lo, beta_c[:, 2 * p:2 * p + 1], beta_c[:, 2 * p + 1:2 * p + 2])
        decay = jnp.where(incl2, jnp.exp(jnp.where(incl2, Gc2 - Gr2, 0.0)), 0.0)
        neg_a.append(jnp.where(strict2, -(beta2 * kkqk[p][0:L] * decay), 0.0))
        p_mat.append((kkqk[p][L:2 * L] * decay).astype(BF16))
        beta_s = stack_cols(beta_c, p)
        eG.append(jnp.exp(stack_cols(G_c, p)))
        k2 = jnp.concatenate([k32[p], k32[p]], axis=0)
        v2 = v_ref[:, 2 * p * GD_DV:(2 * p + 2) * GD_DV].astype(F32)
        v2 = jnp.concatenate([v2[:, 0:GD_DV], v2[:, GD_DV:2 * GD_DV]], axis=0)
        rhs.append(jnp.concatenate([(beta_s * eG[p]) * k2, beta_s * v2], axis=-1))

    n_hi = [a.astype(BF16) for a in neg_a]
    n_bd = [block_diag(x) for x in n_hi]
    m_inv = list(neg_a)
    p32 = [_dot(x, b) for x, b in zip(n_hi, n_bd)]
    for k in range(1, 6):
        pw = [x.astype(BF16) for x in p32]
        pw_bd = [block_diag(x) for x in pw]
        if k < 5:
            both = [_dot(jnp.concatenate([x, m.astype(BF16)], axis=0), b) for x, m, b in zip(pw, m_inv, pw_bd)]
            m_inv = [m + x + b[L:2 * L] for m, x, b in zip(m_inv, p32, both)]
            p32 = [b[0:L] for b in both]
        else:
            m_inv = [m + x + _dot(m.astype(BF16), b) for m, x, b in zip(m_inv, p32, pw_bd)]
    m_bd = [block_diag(m.astype(BF16)) for m in m_inv]
    x1 = [r + _dot(m, r.astype(BF16)) for m, r in zip(m_bd, rhs)]
    res = []
    for r, x, a, ah, ah_bd in zip(rhs, x1, neg_a, n_hi, n_bd):
        al_bd = block_diag((a - ah.astype(F32)).astype(BF16))
        xh = x.astype(BF16)
        xl = (x - xh.astype(F32)).astype(BF16)
        hi = _dot(jnp.concatenate([ah_bd, al_bd], axis=0), xh)
        res.append((r - x) + (hi[0:2 * L] + hi[2 * L:4 * L] + _dot(ah_bd, xl)))
    sol = [x + r + _dot(m, r.astype(BF16)) for x, r, m in zip(x1, res, m_bd)]

    heads = range(nh)
    rows = [slice((hv % 2) * L, (hv % 2 + 1) * L) for hv in heads]
    S = [s_s[hv] for hv in heads]
    Sb = [x.astype(BF16) for x in S]
    wq = [_dot(jnp.concatenate([sol[hv // 2][rows[hv], 0:GD_DK].astype(BF16), qb[hv // 2]], axis=0), Sb[hv])
          for hv in heads]
    ub = [jnp.concatenate([sol[p][rows[2 * p + r], GD_DK:GD_DK + GD_DV] - wq[2 * p + r][0:L] for r in range(2)],
                          axis=0).astype(BF16) for p in pairs]
    pu = [_dot(block_diag(p_mat[p]), ub[p]) for p in pairs]
    kdec = [(jnp.exp(Gl_all[:, hv:hv + 1] - G_c[:, hv:hv + 1]) * k32[hv // 2]).astype(BF16) for hv in heads]
    ku = [_dot_tn(kdec[hv], ub[hv // 2][rows[hv]]) for hv in heads]
    for hv in heads:
        o = eG[hv // 2][rows[hv]] * wq[hv][L:2 * L] + pu[hv // 2][rows[hv]]
        o_ref[:, hv * GD_DV:(hv + 1) * GD_DV] = o.astype(o_ref.dtype)
        s_s[hv] = jnp.exp(Gl_all[:, hv:hv + 1]) * S[hv] + ku[hv]

    @pl.when(jnp.logical_and(is_tail, is_ctx))
    def _():
        sout_ref[...] = s_s[...]


def _gdn_scan(qkv, ab, cache_s, j_layer):
    HV, nh, L, S = GD_V_HEADS, GD_GV, GD_CHUNK, GD_SLOTS
    a_d = ab[:, 0:2 * HV].reshape(T_ALL, 2, GD_GROUPS, nh)
    b_d = ab[:, 2 * HV:4 * HV].reshape(T_ALL, 2, GD_GROUPS, nh)
    gcol = jnp.concatenate([a_d, b_d], axis=-1).reshape(S.n, L, 2, GD_GROUPS, 2 * nh)
    gcol = gcol.transpose(2, 3, 0, 1, 4)
    grow = gcol.transpose(0, 1, 2, 4, 3)

    qb = GD_GQ * GD_DK
    vb = nh * GD_DV
    return pl.pallas_call(
        _gdn_kernel,
        grid=(2, GD_GROUPS, S.n),
        in_specs=[
            pl.BlockSpec((L, qb), lambda d, p, g: (S.slot(d, g), p)),
            pl.BlockSpec((L, qb), lambda d, p, g: (S.slot(d, g), GD_GROUPS + p)),
            pl.BlockSpec((L, vb), lambda d, p, g: (S.slot(d, g), GD_GROUPS + p)),
            pl.BlockSpec((None, None, None, L, 2 * nh), lambda d, p, g: (d, p, S.slot(d, g), 0, 0)),
            pl.BlockSpec((None, None, None, 2 * nh, L), lambda d, p, g: (d, p, S.slot(d, g), 0, 0)),
            pl.BlockSpec((None, None, None, nh, GD_DK, GD_DV),
                         lambda d, p, g: (S.lat_batch(d, g), j_layer, d, p, 0, 0)),
        ],
        out_specs=[
            pl.BlockSpec((None, L, vb), lambda d, p, g: (d, S.slot(d, g), p)),
            pl.BlockSpec((None, None, nh, GD_DK, GD_DV), lambda d, p, g: (S.ctx_seq(d, g), d, p, 0, 0)),
        ],
        out_shape=[
            jax.ShapeDtypeStruct((2, T_ALL, GD_INNER), BF16),
            jax.ShapeDtypeStruct((BATCH, 2, HV, GD_DK, GD_DV), F32),
        ],
        scratch_shapes=[pltpu.VMEM((nh, GD_DK, GD_DV), F32)],
        compiler_params=_cparams(3),
        name="gdn_scan",
    )(qkv, qkv, qkv, gcol, grow, cache_s)


def _gdn_post_kernel(of_ref, ob_ref, z_ref, g_ref, y_ref, *, tn):
    g = g_ref[...]
    for hh in range(tn // GD_DV):
        sl = slice(hh * GD_DV, (hh + 1) * GD_DV)
        o = of_ref[:, sl].astype(F32) + ob_ref[:, sl].astype(F32)
        r = lax.rsqrt(jnp.mean(o * o, axis=-1, keepdims=True) + EPS)
        y_ref[:, sl] = ((o * r * g) * _silu(z_ref[:, sl].astype(F32))).astype(BF16)


def _gdn_post(odir, z, g_norm):
    tm, tn = 1024, 512
    return pl.pallas_call(
        functools.partial(_gdn_post_kernel, tn=tn),
        grid=(T_ALL // tm, GD_INNER // tn),
        in_specs=[
            pl.BlockSpec((None, tm, tn), lambda i, j: (0, i, j)),
            pl.BlockSpec((None, tm, tn), lambda i, j: (1, i, j)),
            pl.BlockSpec((tm, tn), lambda i, j: (i, j)),
            pl.BlockSpec((1, GD_DV), lambda i, j: (0, 0)),
        ],
        out_specs=pl.BlockSpec((tm, tn), lambda i, j: (i, j)),
        out_shape=jax.ShapeDtypeStruct((T_ALL, GD_INNER), BF16),
        compiler_params=_cparams(2),
        name="gdn_gate_norm",
    )(odir, odir, z, g_norm.reshape(1, GD_DV))


def kernel(x_prompt, x_sample, c, cache_ml_C, cache_ml_n, cache_ml_m, cache_gd_S, c_ctx, w_ada, b_ada, g_norm,
           w_ml_in, b_ml_gate, g_ml_head, w_ml_out, w_sc_in, w_sc_conv, w_sc_out, w_gd_in, w_gd_conv, gd_A_log,
           gd_dt_bias, g_gd_norm, w_gd_out, g_final):
    xs = (x_prompt.reshape(T_CTX, D_MODEL), x_sample.reshape(T_LAT, D_MODEL))
    cond = jnp.concatenate([c_ctx[None, :], c, jnp.zeros((N_COND - 1 - DEC_BATCH, D_MODEL), F32)], axis=0)
    mod = _modulation(cond, w_ada, b_ada)
    w_ml_nk = jnp.swapaxes(w_ml_in, 1, 2)

    state_ml_c = None
    ml_n, ml_m, gd_s = [], [], []
    for l in range(DEPTH):
        shift = mod[l, :, 0:D_MODEL]
        scale = mod[l, :, D_MODEL:2 * D_MODEL]
        gate = mod[l, :, 2 * D_MODEL:3 * D_MODEL]
        h = _norm_mod(xs, g_norm[l], shift, scale)
        j = l // 3
        kind = l % 3
        if kind == 0:
            proj = _project(h, w_ml_nk, j, 0, ML_MAIN, 1024, BF16, w_is_nk=True)
            gates = _ml_gates(h, w_ml_nk, j, b_ml_gate[j])
            hdir, state_ml_c, n_fin, m_fin = _mlstm_scan(proj, gates, cache_ml_C, cache_ml_n, cache_ml_m, j,
                                                         state_ml_c)
            ml_n.append(n_fin)
            ml_m.append(m_fin[..., 0])
            y = _mlstm_post(hdir, proj, g_ml_head[j])
            xs = (_project_residual(y, w_ml_out, j, xs, gate),)
        elif kind == 1:
            y = _shortconv(h, w_sc_in, j, w_sc_conv)
            xs = (_project_residual(y, w_sc_out, j, xs, gate),)
        else:
            qkv = _gdn_qkv(h, w_gd_in, j, w_gd_conv)
            z = _project(h, w_gd_in, j, GD_CONV_CH, GD_INNER, 1024, BF16)
            ab = _gd_gates(h, w_gd_in, j, gd_A_log[j], gd_dt_bias[j])
            odir, s_fin = _gdn_scan(qkv, ab, cache_gd_S, j)
            gd_s.append(s_fin)
            y = _gdn_post(odir, z, g_gd_norm[j])
            xs = (_project_residual(y, w_gd_out, j, xs, gate),)

    y_ctx, y_lat = _final_norm(xs[0], g_final)
    y_prompt = y_ctx.reshape(BATCH, SEQ, D_MODEL)
    y_sample = y_lat.reshape(DEC_BATCH, DEC_SEQ, D_MODEL)
    state_ml_n = jnp.stack(ml_n, axis=1)
    state_ml_m = jnp.stack(ml_m, axis=1)
    state_gd_s = jnp.stack(gd_s, axis=1)
    return (y_prompt, y_sample, state_ml_c, state_ml_n, state_ml_m, state_gd_s)
```
